```python
import jax, jax.numpy as jnp
from jax import lax
import numpy as np

D_MODEL = 1024
BATCH = 4
SEQ = 4096
DEPTH = 1
DEC_BATCH = 8
DEC_SEQ = 32
PAST_LEN = 2048

CHUNK = 64
RET_HEADS = 4
RET_HEAD_DIM = 128
RET_WIDTH = RET_HEADS * RET_HEAD_DIM
CONV_WIDTH = D_MODEL - RET_WIDTH
MIX_WIDTH = RET_WIDTH + CONV_WIDTH
IN_WIDTH = 4 * RET_WIDTH + 2 * CONV_WIDTH
CONV_KERNEL = 31
ROPE_BASE = 10000.0
PEER_HEADS = 8
PEER_KEYS = 128
PEER_EXPERTS = PEER_KEYS * PEER_KEYS
PEER_TOPK = 16
PEER_KEY_DIM = 128
PEER_BLOCK = 256
EPS = 1e-6

kernel_name = 'retention_conformer_peer_stream'


def _rmsnorm(x, gain):
    xf = x.astype(jnp.float32)
    y = xf * lax.rsqrt(jnp.mean(xf * xf, axis=-1, keepdims=True) + EPS)
    return (y * gain.astype(jnp.float32)).astype(x.dtype)


def _rotary(t, pos):
    half = t.shape[-1] // 2
    inv_freq = ROPE_BASE ** (-jnp.arange(half, dtype=jnp.float32) / half)
    ang = pos[:, None] * inv_freq[None, :]
    cos, sin = jnp.cos(ang), jnp.sin(ang)
    t1, t2 = t[..., :half], t[..., half:]
    return jnp.concatenate([t1 * cos - t2 * sin, t1 * sin + t2 * cos], axis=-1)


def _retention_decays():
    return jnp.log1p(-jnp.exp2(-5.0 - jnp.arange(RET_HEADS, dtype=jnp.float32)))


def _retention_block(q, k, v, s, log_gamma):
    L = q.shape[2]
    idx = jnp.arange(L, dtype=jnp.float32)
    diff = idx[:, None] - idx[None, :]
    causal = diff >= 0
    decay = jnp.where(causal[None], jnp.exp(log_gamma[:, None, None] * jnp.where(causal, diff, 0.0)[None]), 0.0)
    scores = jnp.einsum('bhld,bhmd->bhlm', q, k) * decay[None]
    cross = jnp.exp(log_gamma[:, None] * (idx[None, :] + 1.0))
    out = (jnp.einsum('bhlm,bhme->bhle', scores, v)
           + jnp.einsum('bhld,bhde->bhle', q, s) * cross[None, :, :, None])
    k_w = k * jnp.exp(log_gamma[:, None] * (L - 1.0 - idx[None, :]))[None, :, :, None]
    s_new = jnp.exp(log_gamma * L)[None, :, None, None] * s + jnp.einsum('bhld,bhle->bhde', k_w, v)
    return out, s_new


def _token_mixers(h, pos, ret_state, conv_buf, w_in, ret_gn_gain, conv_dw, conv_dw_bias,
                  conv_ln_gain, conv_ln_bias, w_out):
    B, L, _ = h.shape
    z = h @ w_in
    q, k, v, g, a, b = jnp.split(
        z, [RET_WIDTH, 2 * RET_WIDTH, 3 * RET_WIDTH, 4 * RET_WIDTH, 4 * RET_WIDTH + CONV_WIDTH], axis=-1)

    def heads(t):
        return t.reshape(B, L, RET_HEADS, RET_HEAD_DIM).transpose(0, 2, 1, 3).astype(jnp.float32)
    q = _rotary(heads(q), pos)
    k = _rotary(heads(k), pos) * (RET_HEAD_DIM ** -0.5)
    v = heads(v)
    blk = min(CHUNK, L)
    n_blk = L // blk

    def to_blocks(t):
        return t.reshape(B, RET_HEADS, n_blk, blk, RET_HEAD_DIM).transpose(2, 0, 1, 3, 4)
    log_gamma = _retention_decays()

    def step(s, qkv):
        o_blk, s = _retention_block(qkv[0], qkv[1], qkv[2], s, log_gamma)
        return s, o_blk
    s_new, o = lax.scan(step, ret_state.astype(jnp.float32), (to_blocks(q), to_blocks(k), to_blocks(v)))
    o = o.transpose(1, 2, 0, 3, 4).reshape(B, RET_HEADS, L, RET_HEAD_DIM)
    oc = o - jnp.mean(o, axis=-1, keepdims=True)
    o = oc * lax.rsqrt(jnp.mean(oc * oc, axis=-1, keepdims=True) + EPS)
    o = o.transpose(0, 2, 1, 3).reshape(B, L, RET_WIDTH) * ret_gn_gain.astype(jnp.float32)
    y_ret = jax.nn.silu(g.astype(jnp.float32)) * o

    u = a * jax.nn.sigmoid(b)
    ext = jnp.concatenate([conv_buf.astype(u.dtype), u], axis=1)
    c = lax.conv_general_dilated(ext, conv_dw[:, None, :].astype(u.dtype), window_strides=(1,),
                                 padding='VALID', dimension_numbers=('NWC', 'WIO', 'NWC'),
                                 feature_group_count=CONV_WIDTH)
    c = c.astype(jnp.float32) + conv_dw_bias.astype(jnp.float32)
    cm = c - jnp.mean(c, axis=-1, keepdims=True)
    c = (cm * lax.rsqrt(jnp.mean(cm * cm, axis=-1, keepdims=True) + EPS)
         * conv_ln_gain.astype(jnp.float32) + conv_ln_bias.astype(jnp.float32))
    y_conv = jax.nn.silu(c)

    y = jnp.concatenate([y_ret, y_conv], axis=-1).astype(h.dtype) @ w_out
    new_buf = ext[:, ext.shape[1] - (CONV_KERNEL - 1):]
    return y, s_new.astype(h.dtype), new_buf


def _peer(x2d, peer_query, peer_sub_keys, peer_down, peer_up):
    T, D = x2d.shape
    n_blocks = -(-T // PEER_BLOCK)
    xp = jnp.pad(x2d, ((0, n_blocks * PEER_BLOCK - T), (0, 0)))
    keys = peer_sub_keys.astype(jnp.float32)

    def block(xb):
        P = xb.shape[0]
        qry = (xb @ peer_query).reshape(P, PEER_HEADS, 2, PEER_KEY_DIM).astype(jnp.float32)
        scores = jnp.einsum('phsd,hsnd->phsn', qry, keys)
        top_s, top_i = lax.top_k(scores, PEER_TOPK)
        cand = top_s[:, :, 0, :, None] + top_s[:, :, 1, None, :]
        best_s, best_c = lax.top_k(cand.reshape(P, PEER_HEADS, PEER_TOPK * PEER_TOPK), PEER_TOPK)
        ia = jnp.take_along_axis(top_i[:, :, 0], best_c // PEER_TOPK, axis=-1)
        ib = jnp.take_along_axis(top_i[:, :, 1], best_c % PEER_TOPK, axis=-1)
        expert = (ia * PEER_KEYS + ib).reshape(P, PEER_HEADS * PEER_TOPK)
        gate = jax.nn.softmax(best_s, axis=-1).reshape(P, PEER_HEADS * PEER_TOPK)
        u_rows = peer_down[expert]
        act = jax.nn.gelu(jnp.einsum('pd,ped->pe', xb, u_rows).astype(jnp.float32))
        v_rows = peer_up[expert]
        return jnp.einsum('pe,ped->pd', (gate * act).astype(xb.dtype), v_rows)

    out = lax.map(block, xp.reshape(n_blocks, PEER_BLOCK, D))
    return out.reshape(n_blocks * PEER_BLOCK, D)[:T]


def _layer(x, pos, ret_state, conv_buf, norm_mix, w_in, ret_gn_gain, conv_dw, conv_dw_bias,
           conv_ln_gain, conv_ln_bias, w_out, norm_ffn, peer_query, peer_sub_keys, peer_down, peer_up):
    y, s_new, buf_new = _token_mixers(_rmsnorm(x, norm_mix), pos, ret_state, conv_buf, w_in, ret_gn_gain,
                                      conv_dw, conv_dw_bias, conv_ln_gain, conv_ln_bias, w_out)
    x = x + y
    h = _rmsnorm(x, norm_ffn)
    x = x + _peer(h.reshape(-1, x.shape[-1]), peer_query, peer_sub_keys, peer_down, peer_up).reshape(x.shape)
    return x, s_new, buf_new


def setup_inputs(seed: int = 0) -> dict:
    key = jax.random.key(seed)
    ks = jax.random.split(key, 20)
    f32 = jnp.float32
    nrm = lambda k, shape, scale: scale * jax.random.normal(k, shape, f32)
    return {
        'x_prompt': nrm(ks[0], (BATCH, SEQ, D_MODEL), 1.0),
        'x_sample': nrm(ks[1], (DEC_BATCH, DEC_SEQ, D_MODEL), 1.0),
        'state_retention': nrm(ks[2], (DEPTH, DEC_BATCH, RET_HEADS, RET_HEAD_DIM, RET_HEAD_DIM), 1.0),
        'state_conv': nrm(ks[3], (DEPTH, DEC_BATCH, CONV_KERNEL - 1, CONV_WIDTH), 0.5),
        'norm_mix': 1.0 + nrm(ks[4], (DEPTH, D_MODEL), 0.02),
        'w_in': nrm(ks[5], (DEPTH, D_MODEL, IN_WIDTH), D_MODEL ** -0.5),
        'ret_gn_gain': 1.0 + nrm(ks[6], (DEPTH, RET_WIDTH), 0.02),
        'conv_dw': nrm(ks[7], (DEPTH, CONV_KERNEL, CONV_WIDTH), CONV_KERNEL ** -0.5),
        'conv_dw_bias': nrm(ks[8], (DEPTH, CONV_WIDTH), 0.02),
        'conv_ln_gain': 1.0 + nrm(ks[9], (DEPTH, CONV_WIDTH), 0.02),
        'conv_ln_bias': nrm(ks[10], (DEPTH, CONV_WIDTH), 0.02),
        'w_out': nrm(ks[11], (DEPTH, MIX_WIDTH, D_MODEL), MIX_WIDTH ** -0.5),
        'norm_ffn': 1.0 + nrm(ks[12], (DEPTH, D_MODEL), 0.02),
        'peer_query': nrm(ks[13], (DEPTH, D_MODEL, PEER_HEADS * 2 * PEER_KEY_DIM), D_MODEL ** -0.5),
        'peer_sub_keys': nrm(ks[14], (DEPTH, PEER_HEADS, 2, PEER_KEYS, PEER_KEY_DIM), PEER_KEY_DIM ** -0.5),
        'peer_down': nrm(ks[15], (DEPTH, PEER_EXPERTS, D_MODEL), D_MODEL ** -0.5),
        'peer_up': nrm(ks[16], (DEPTH, PEER_EXPERTS, D_MODEL), PEER_HEADS ** -0.5),
        'norm_final': 1.0 + nrm(ks[17], (D_MODEL,), 0.02),
    }


def reference(x_prompt, x_sample, state_retention, state_conv, norm_mix, w_in, ret_gn_gain, conv_dw,
              conv_dw_bias, conv_ln_gain, conv_ln_bias, w_out, norm_ffn, peer_query, peer_sub_keys,
              peer_down, peer_up, norm_final):
    bp, lp, _ = x_prompt.shape
    ls = x_sample.shape[1]
    pos_p = jnp.arange(lp, dtype=jnp.float32)
    pos_s = PAST_LEN + jnp.arange(ls, dtype=jnp.float32)
    ret0 = jnp.zeros((bp, RET_HEADS, RET_HEAD_DIM, RET_HEAD_DIM), x_prompt.dtype)
    conv0 = jnp.zeros((bp, CONV_KERNEL - 1, CONV_WIDTH), x_prompt.dtype)
    xp, xs = x_prompt, x_sample
    ret_p, ret_s, conv_p, conv_s = [], [], [], []
    for l in range(DEPTH):
        weights = (norm_mix[l], w_in[l], ret_gn_gain[l], conv_dw[l], conv_dw_bias[l], conv_ln_gain[l],
                   conv_ln_bias[l], w_out[l], norm_ffn[l], peer_query[l], peer_sub_keys[l],
                   peer_down[l], peer_up[l])
        xp, r, c = _layer(xp, pos_p, ret0, conv0, *weights)
        ret_p.append(r)
        conv_p.append(c)
        xs, r, c = _layer(xs, pos_s, state_retention[l], state_conv[l], *weights)
        ret_s.append(r)
        conv_s.append(c)
    y_prompt = _rmsnorm(xp, norm_final)
    y_sample = _rmsnorm(xs, norm_final)
    return (y_prompt, y_sample, jnp.stack(ret_p), jnp.stack(ret_s), jnp.stack(conv_p), jnp.stack(conv_s))
```

```python
import functools

import numpy as np
import jax
import jax.numpy as jnp
from jax import lax
from jax.experimental import pallas as pl
from jax.experimental.pallas import tpu as pltpu

D_MODEL = 1024
RET_HEADS = 4
RET_HEAD_DIM = 128
RET_WIDTH = RET_HEADS * RET_HEAD_DIM
CONV_WIDTH = D_MODEL - RET_WIDTH
IN_WIDTH = 4 * RET_WIDTH + 2 * CONV_WIDTH
CONV_KERNEL = 31
CONV_HIST = CONV_KERNEL - 1
ROPE_BASE = 10000.0
PEER_HEADS = 8
PEER_KEYS = 128
PEER_TOPK = 16
PEER_KEY_DIM = 128
PEER_EXPERTS = PEER_KEYS * PEER_KEYS
EPS = 1e-6
PAST_LEN = 2048

LANES = 128
SUBLANES = 8
HIST_BASE = 32
CONV_ROWS = 32
VMEM_LIMIT = 56 * 1024 * 1024

F32 = jnp.float32
BF16 = jnp.bfloat16
NT_DIMS = (((1,), (1,)), ((), ()))
TN_DIMS = (((0,), (0,)), ((), ()))


def _rms(x, gain):
    return x * lax.rsqrt(jnp.mean(x * x, axis=-1, keepdims=True) + EPS) * gain


def _center_norm(x):
    xc = x - jnp.mean(x, axis=-1, keepdims=True)
    return xc * lax.rsqrt(jnp.mean(xc * xc, axis=-1, keepdims=True) + EPS)


def _mixer_kernel(x_ref, ret0_ref, conv0_ref, nmix_ref, win_ref, cq_ref, sq_ref, ck_ref, sk_ref,
                  dmask_ref, cross_ref, kdec_ref, gn_ref, cw_ref, cb_ref, lng_ref, lnb_ref, wout_ref,
                  x1_ref, retn_ref, convn_ref,
                  z_ref, s_ref, ext_ref, ycat_ref, *, tm, chunk, sdecay):
    i = pl.program_id(1)
    last = pl.num_programs(1) - 1

    @pl.when(i == 0)
    def _():
        s_ref[...] = ret0_ref[...]
        ext_ref[HIST_BASE - CONV_HIST:HIST_BASE, :] = conv0_ref[...]

    x = x_ref[...]
    h = _rms(x, nmix_ref[...])
    z_ref[...] = jnp.dot(h.astype(BF16), win_ref[...], preferred_element_type=F32)

    for c in range(tm // chunk):
        rows = slice(c * chunk, (c + 1) * chunk)
        cq, sq, ck, sk = cq_ref[rows, :], sq_ref[rows, :], ck_ref[rows, :], sk_ref[rows, :]
        for hd in range(RET_HEADS):
            lo = hd * RET_HEAD_DIM
            q = z_ref[rows, lo:lo + RET_HEAD_DIM]
            k = z_ref[rows, RET_WIDTH + lo:RET_WIDTH + lo + RET_HEAD_DIM]
            v = z_ref[rows, 2 * RET_WIDTH + lo:2 * RET_WIDTH + lo + RET_HEAD_DIM]
            g = z_ref[rows, 3 * RET_WIDTH + lo:3 * RET_WIDTH + lo + RET_HEAD_DIM]
            qr = q * cq + pltpu.roll(q, RET_HEAD_DIM // 2, 1) * sq
            kr = k * ck + pltpu.roll(k, RET_HEAD_DIM // 2, 1) * sk
            qb, vb = qr.astype(BF16), v.astype(BF16)
            sc = lax.dot_general(qb, kr.astype(BF16), NT_DIMS, preferred_element_type=F32) * dmask_ref[hd]
            s_prev = s_ref[hd]
            o = (jnp.dot(sc.astype(BF16), vb, preferred_element_type=F32)
                 + jnp.dot(qb, s_prev.astype(BF16), preferred_element_type=F32) * cross_ref[hd])
            kw = (kr * kdec_ref[hd]).astype(BF16)
            s_ref[hd] = sdecay[hd] * s_prev + lax.dot_general(kw, vb, TN_DIMS, preferred_element_type=F32)
            on = _center_norm(o) * gn_ref[:, lo:lo + RET_HEAD_DIM]
            ycat_ref[rows, lo:lo + RET_HEAD_DIM] = (g * jax.nn.sigmoid(g) * on).astype(BF16)

    a = z_ref[:, 4 * RET_WIDTH:4 * RET_WIDTH + CONV_WIDTH]
    b = z_ref[:, 4 * RET_WIDTH + CONV_WIDTH:IN_WIDTH]
    ext_ref[HIST_BASE:HIST_BASE + tm, :] = a * jax.nn.sigmoid(b)
    first = HIST_BASE - CONV_HIST
    for r in range(tm // CONV_ROWS):
        acc = jnp.broadcast_to(cb_ref[...], (CONV_ROWS, CONV_WIDTH))
        for j in range(CONV_KERNEL):
            start = first + r * CONV_ROWS + j
            acc = acc + cw_ref[j:j + 1, :] * ext_ref[start:start + CONV_ROWS, :]
        cn = _center_norm(acc) * lng_ref[...] + lnb_ref[...]
        ycat_ref[r * CONV_ROWS:(r + 1) * CONV_ROWS, RET_WIDTH:] = (cn * jax.nn.sigmoid(cn)).astype(BF16)
    hist = ext_ref[first + tm:HIST_BASE + tm, :]
    ext_ref[first:HIST_BASE, :] = hist

    @pl.when(i == last)
    def _():
        convn_ref[...] = hist
        retn_ref[...] = s_ref[...]

    x1_ref[...] = x + jnp.dot(ycat_ref[...], wout_ref[...], preferred_element_type=F32)


def _const_spec(shape):
    return pl.BlockSpec(shape, lambda b, i: (0,) * len(shape))


def _mixer(x, ret0, conv0, pos0, w, *, tm, chunk):
    bsz, seq, _ = x.shape
    assert seq % tm == 0 and tm % chunk == 0 and tm % CONV_ROWS == 0 and tm >= CONV_HIST
    half = RET_HEAD_DIM // 2
    inv_freq = ROPE_BASE ** (-jnp.arange(half, dtype=F32) / half)
    ang = (pos0 + jnp.arange(seq, dtype=F32))[:, None] * inv_freq[None, :]
    cos, sin = jnp.cos(ang), jnp.sin(ang)
    cos2 = jnp.concatenate([cos, cos], axis=-1)
    sin2 = jnp.concatenate([-sin, sin], axis=-1)
    kscale = RET_HEAD_DIM ** -0.5
    log_gamma = jnp.log1p(-jnp.exp2(-5.0 - jnp.arange(RET_HEADS, dtype=F32)))
    idx = jnp.arange(chunk, dtype=F32)
    diff = idx[:, None] - idx[None, :]
    causal = diff >= 0
    dmask = jnp.where(causal[None], jnp.exp(log_gamma[:, None, None] * jnp.where(causal, diff, 0.0)[None]), 0.0)
    cross = jnp.broadcast_to(jnp.exp(log_gamma[:, None] * (idx[None, :] + 1.0))[:, :, None],
                             (RET_HEADS, chunk, RET_HEAD_DIM))
    kdec = jnp.broadcast_to(jnp.exp(log_gamma[:, None] * (chunk - 1.0 - idx[None, :]))[:, :, None],
                            (RET_HEADS, chunk, RET_HEAD_DIM))
    lg64 = np.log1p(-np.exp2(-5.0 - np.arange(RET_HEADS, dtype=np.float64)))
    sdecay = tuple(float(np.float32(np.exp(lg64[h] * chunk))) for h in range(RET_HEADS))

    tok_spec = pl.BlockSpec((None, tm, D_MODEL), lambda b, i: (b, i, 0))
    tab_spec = pl.BlockSpec((tm, RET_HEAD_DIM), lambda b, i: (i, 0))
    ret_spec = pl.BlockSpec((None, RET_HEADS, RET_HEAD_DIM, RET_HEAD_DIM), lambda b, i: (b, 0, 0, 0))
    conv_spec = pl.BlockSpec((None, CONV_HIST, CONV_WIDTH), lambda b, i: (b, 0, 0))
    return pl.pallas_call(
        functools.partial(_mixer_kernel, tm=tm, chunk=chunk, sdecay=sdecay),
        grid=(bsz, seq // tm),
        in_specs=[tok_spec, ret_spec, conv_spec,
                  _const_spec((1, D_MODEL)), _const_spec((D_MODEL, IN_WIDTH)),
                  tab_spec, tab_spec, tab_spec, tab_spec,
                  _const_spec((RET_HEADS, chunk, chunk)),
                  _const_spec((RET_HEADS, chunk, RET_HEAD_DIM)), _const_spec((RET_HEADS, chunk, RET_HEAD_DIM)),
                  _const_spec((1, RET_WIDTH)), _const_spec((CONV_KERNEL, CONV_WIDTH)),
                  _const_spec((1, CONV_WIDTH)), _const_spec((1, CONV_WIDTH)), _const_spec((1, CONV_WIDTH)),
                  _const_spec((D_MODEL, D_MODEL))],
        out_specs=[tok_spec, ret_spec, conv_spec],
        out_shape=[jax.ShapeDtypeStruct((bsz, seq, D_MODEL), F32),
                   jax.ShapeDtypeStruct((bsz, RET_HEADS, RET_HEAD_DIM, RET_HEAD_DIM), F32),
                   jax.ShapeDtypeStruct((bsz, CONV_HIST, CONV_WIDTH), F32)],
        scratch_shapes=[pltpu.VMEM((tm, IN_WIDTH), F32),
                        pltpu.VMEM((RET_HEADS, RET_HEAD_DIM, RET_HEAD_DIM), F32),
                        pltpu.VMEM((HIST_BASE + tm, CONV_WIDTH), F32),
                        pltpu.VMEM((tm, D_MODEL), BF16)],
        compiler_params=pltpu.CompilerParams(dimension_semantics=("arbitrary", "arbitrary"),
                                             vmem_limit_bytes=VMEM_LIMIT),
        name="mixer",
    )(x, ret0, conv0, w["norm_mix"], w["w_in"], cos2, sin2, cos2 * kscale, sin2 * kscale,
      dmask, cross, kdec, w["ret_gn_gain"], w["conv_dw"], w["conv_dw_bias"], w["conv_ln_gain"],
      w["conv_ln_bias"], w["w_out"])


def _top16(sc, iota):
    s = sc
    rank = jnp.full(sc.shape, float(PEER_TOPK), F32)
    tops = []
    for k in range(PEER_TOPK):
        m = jnp.max(s, axis=0, keepdims=True)
        idx = jnp.min(jnp.where(s == m, iota, float(PEER_KEYS)), axis=0, keepdims=True)
        hit = iota == idx
        s = jnp.where(hit, -jnp.inf, s)
        rank = jnp.where(hit, float(k), rank)
        tops.append(m)
    return jnp.concatenate(tops, axis=0), rank


def _merge16(a, b, rank_a, iota16):
    count = jnp.zeros_like(a)
    front = a + b[0:1, :]
    taken = jnp.zeros(rank_a.shape, F32)
    best0 = None
    norm = None
    for k in range(PEER_TOPK):
        m = jnp.max(front, axis=0, keepdims=True)
        isel = jnp.min(jnp.where(front == m, iota16, float(PEER_TOPK)), axis=0, keepdims=True)
        hit = iota16 == isel
        if k == 0:
            best0 = m
            norm = jnp.ones_like(m)
        else:
            norm = norm + jnp.exp(m - best0)
        count = count + jnp.where(hit, 1.0, 0.0)
        taken = taken + jnp.where(rank_a == isel, 1.0, 0.0)
        if k + 1 < PEER_TOPK:
            nxt = jnp.sum(jnp.where(hit, count, 0.0), axis=0, keepdims=True)
            b_nxt = jnp.sum(jnp.where(iota16 == nxt, b, 0.0), axis=0, keepdims=True)
            b_nxt = jnp.where(nxt >= float(PEER_TOPK), -jnp.inf, b_nxt)
            front = jnp.where(hit, a + b_nxt, front)
    return taken, norm


def _select_kernel(x1_ref, nffn_ref, wqT_ref, keys_ref, h2_ref, ra_ref, ea_ref, rb_ref, eb_ref, qT_ref, *, tm):
    h2 = _rms(x1_ref[...], nffn_ref[...]).astype(BF16)
    h2_ref[...] = h2
    qT_ref[...] = lax.dot_general(wqT_ref[...], h2, NT_DIMS, preferred_element_type=F32)
    iota = lax.broadcasted_iota(jnp.int32, (PEER_KEYS, LANES), 0).astype(F32)
    iota16 = lax.broadcasted_iota(jnp.int32, (PEER_TOPK, LANES), 0).astype(F32)

    for sb in range(tm // LANES):
        lanes = slice(sb * LANES, (sb + 1) * LANES)

        def head_body(hd, carry):
            sc, top, rank = [], [], []
            for s in range(2):
                row0 = pl.multiple_of((hd * 2 + s) * PEER_KEY_DIM, PEER_KEY_DIM)
                qs = qT_ref[pl.ds(row0, PEER_KEY_DIM), lanes].astype(BF16)
                sc_s = jnp.dot(keys_ref[hd * 2 + s], qs, preferred_element_type=F32)
                top_s, rank_s = _top16(sc_s, iota)
                sc.append(sc_s)
                top.append(top_s)
                rank.append(rank_s)
            taken, norm = _merge16(top[0], top[1], rank[0], iota16)
            ra_ref[hd, :, lanes] = taken
            ea_ref[hd, :, lanes] = jnp.exp(sc[0] - top[0][0:1, :]) / norm
            rb_ref[hd, :, lanes] = rank[1]
            eb_ref[hd, :, lanes] = jnp.exp(sc[1] - top[1][0:1, :])
            return carry

        lax.fori_loop(0, PEER_HEADS, head_body, 0)


def _select(x1, w, *, tm):
    t = x1.shape[0]
    assert t % tm == 0 and tm % LANES == 0
    tab_shape = jax.ShapeDtypeStruct((PEER_HEADS, PEER_KEYS, t), F32)
    tab_spec = pl.BlockSpec((PEER_HEADS, PEER_KEYS, tm), lambda i: (0, 0, i))
    tok_spec = pl.BlockSpec((tm, D_MODEL), lambda i: (i, 0))
    nq = PEER_HEADS * 2 * PEER_KEY_DIM
    return pl.pallas_call(
        functools.partial(_select_kernel, tm=tm),
        grid=(t // tm,),
        in_specs=[tok_spec,
                  pl.BlockSpec((1, D_MODEL), lambda i: (0, 0)),
                  pl.BlockSpec((nq, D_MODEL), lambda i: (0, 0)),
                  pl.BlockSpec((PEER_HEADS * 2, PEER_KEYS, PEER_KEY_DIM), lambda i: (0, 0, 0))],
        out_specs=[tok_spec, tab_spec, tab_spec, tab_spec, tab_spec],
        out_shape=[jax.ShapeDtypeStruct((t, D_MODEL), BF16), tab_shape, tab_shape, tab_shape, tab_shape],
        scratch_shapes=[pltpu.VMEM((nq, tm), F32)],
        compiler_params=pltpu.CompilerParams(dimension_semantics=("arbitrary",), vmem_limit_bytes=VMEM_LIMIT),
        name="select",
    )(x1, w["norm_ffn"], w["peer_query_t"], w["peer_keys"])


def _peer_kernel(h2_ref, x1_ref, ra_ref, ea_ref, rb_ref, eb_ref, down_ref, upT_ref, nfin_ref, y_ref,
                 act_ref, wT_ref, acc_ref, *, tm, te):
    j = pl.program_id(1)
    n_first = te // PEER_KEYS

    @pl.when(j == 0)
    def _():
        acc_ref[...] = jnp.zeros_like(acc_ref)

    act_ref[...] = lax.dot_general(down_ref[...], h2_ref[...], NT_DIMS, preferred_element_type=F32)
    for sb in range(tm // LANES):
        lanes = slice(sb * LANES, (sb + 1) * LANES)
        for grp in range(n_first // SUBLANES):
            first0 = pl.multiple_of(j * n_first + grp * SUBLANES, SUBLANES)
            ra_grp = [ra_ref[hd, pl.ds(first0, SUBLANES), lanes] for hd in range(PEER_HEADS)]
            ea_grp = [ea_ref[hd, pl.ds(first0, SUBLANES), lanes] for hd in range(PEER_HEADS)]
            for a in range(SUBLANES):
                row0 = (grp * SUBLANES + a) * PEER_KEYS
                rows = slice(row0, row0 + PEER_KEYS)
                gate = jnp.zeros((PEER_KEYS, LANES), F32)
                for hd in range(PEER_HEADS):
                    hit = rb_ref[hd, :, lanes] < ra_grp[hd][a:a + 1, :]
                    gate = gate + jnp.where(hit, eb_ref[hd, :, lanes], 0.0) * ea_grp[hd][a:a + 1, :]
                wT_ref[rows, lanes] = (gate * jax.nn.gelu(act_ref[rows, lanes])).astype(BF16)
    acc_ref[...] += jnp.dot(upT_ref[...], wT_ref[...], preferred_element_type=F32)

    @pl.when(j == pl.num_programs(1) - 1)
    def _():
        y_ref[...] = _rms(x1_ref[...] + acc_ref[...].T, nfin_ref[...])


def _peer(h2, x1, tabs, w, *, tm, te):
    t = x1.shape[0]
    assert t % tm == 0 and tm % LANES == 0 and PEER_EXPERTS % te == 0 and te % (SUBLANES * PEER_KEYS) == 0
    tok_spec = pl.BlockSpec((tm, D_MODEL), lambda i, j: (i, 0))
    tab_spec = pl.BlockSpec((PEER_HEADS, PEER_KEYS, tm), lambda i, j: (0, 0, i))
    return pl.pallas_call(
        functools.partial(_peer_kernel, tm=tm, te=te),
        grid=(t // tm, PEER_EXPERTS // te),
        in_specs=[tok_spec, tok_spec, tab_spec, tab_spec, tab_spec, tab_spec,
                  pl.BlockSpec((te, D_MODEL), lambda i, j: (j, 0)),
                  pl.BlockSpec((D_MODEL, te), lambda i, j: (0, j)),
                  pl.BlockSpec((1, D_MODEL), lambda i, j: (0, 0))],
        out_specs=tok_spec,
        out_shape=jax.ShapeDtypeStruct((t, D_MODEL), F32),
        scratch_shapes=[pltpu.VMEM((te, tm), F32), pltpu.VMEM((te, tm), BF16), pltpu.VMEM((D_MODEL, tm), F32)],
        compiler_params=pltpu.CompilerParams(dimension_semantics=("arbitrary", "arbitrary"),
                                             vmem_limit_bytes=VMEM_LIMIT),
        name="peer",
    )(h2, x1, *tabs, w["peer_down"], w["peer_up_t"], w["norm_final"])


def _ffn(x1, w, *, tm, te):
    shape = x1.shape
    x1 = x1.reshape(-1, D_MODEL)
    h2, *tabs = _select(x1, w, tm=min(tm, 256))
    return _peer(h2, x1, tabs, w, tm=tm, te=te).reshape(shape)


def kernel(x_prompt, x_sample, state_retention, state_conv, norm_mix, w_in, ret_gn_gain, conv_dw, conv_dw_bias,
           conv_ln_gain, conv_ln_bias, w_out, norm_ffn, peer_query, peer_sub_keys, peer_down, peer_up, norm_final):
    assert w_in.shape[0] == 1, "single-layer model"
    row = lambda v: v.reshape(1, -1).astype(F32)
    w = {
        "norm_mix": row(norm_mix[0]), "w_in": w_in[0].astype(BF16), "ret_gn_gain": row(ret_gn_gain[0]),
        "conv_dw": conv_dw[0].astype(F32), "conv_dw_bias": row(conv_dw_bias[0]),
        "conv_ln_gain": row(conv_ln_gain[0]), "conv_ln_bias": row(conv_ln_bias[0]),
        "w_out": w_out[0].astype(BF16), "norm_ffn": row(norm_ffn[0]),
        "peer_query_t": peer_query[0].T.astype(BF16),
        "peer_keys": peer_sub_keys[0].reshape(PEER_HEADS * 2, PEER_KEYS, PEER_KEY_DIM).astype(BF16),
        "peer_down": peer_down[0].astype(BF16), "peer_up_t": peer_up[0].T.astype(BF16),
        "norm_final": row(norm_final),
    }
    bp = x_prompt.shape[0]
    ret0 = jnp.zeros((bp, RET_HEADS, RET_HEAD_DIM, RET_HEAD_DIM), F32)
    conv0 = jnp.zeros((bp, CONV_HIST, CONV_WIDTH), F32)
    x1p, ret_p, conv_p = _mixer(x_prompt, ret0, conv0, 0.0, w, tm=512, chunk=256)
    x1s, ret_s, conv_s = _mixer(x_sample, state_retention[0], state_conv[0], float(PAST_LEN), w,
                                tm=x_sample.shape[1], chunk=x_sample.shape[1])
    y_prompt = _ffn(x1p, w, tm=512, te=1024)
    y_sample = _ffn(x1s, w, tm=256, te=1024)
    return (y_prompt, y_sample, ret_p[None], ret_s[None], conv_p[None], conv_s[None])
```

```python
import functools

import numpy as np
import jax
import jax.numpy as jnp
from jax import lax
from jax.experimental import pallas as pl
from jax.experimental.pallas import tpu as pltpu

D_MODEL = 1024
RET_HEADS = 4
RET_HEAD_DIM = 128
RET_WIDTH = RET_HEADS * RET_HEAD_DIM
CONV_WIDTH = D_MODEL - RET_WIDTH
IN_WIDTH = 4 * RET_WIDTH + 2 * CONV_WIDTH
CONV_KERNEL = 31
CONV_HIST = CONV_KERNEL - 1
ROPE_BASE = 10000.0
PEER_HEADS = 8
PEER_KEYS = 128
PEER_TOPK = 16
PEER_KEY_DIM = 128
PEER_EXPERTS = PEER_KEYS * PEER_KEYS
EPS = 1e-6
PAST_LEN = 2048

LANES = 128
SUBLANES = 8
BF16_ROWS = 16
GROUP_FIRST = 4
N_QUERY = PEER_HEADS * 2 * PEER_KEY_DIM
HIST_BASE = 32
CONV_ROWS = 32
VMEM_LIMIT = 56 * 1024 * 1024

F32 = jnp.float32
BF16 = jnp.bfloat16
NT_DIMS = (((1,), (1,)), ((), ()))
TN_DIMS = (((0,), (0,)), ((), ()))


def _rms(x, gain):
    return x * lax.rsqrt(jnp.mean(x * x, axis=-1, keepdims=True) + EPS) * gain


def _center_norm(x):
    xc = x - jnp.mean(x, axis=-1, keepdims=True)
    return xc * lax.rsqrt(jnp.mean(xc * xc, axis=-1, keepdims=True) + EPS)


def _gelu_tanh(x):
    k1 = float(np.sqrt(2.0 / np.pi))
    k2 = k1 * 0.044715
    return (0.5 * x) * (1.0 + jnp.tanh(x * (k1 + k2 * (x * x))))


def _mixer_kernel(x_ref, ret0_ref, conv0_ref, nmix_ref, win_ref, cq_ref, sq_ref, ck_ref, sk_ref,
                  dmask_ref, cross_ref, kdec_ref, gn_ref, cw_ref, cb_ref, lng_ref, lnb_ref, wout_ref,
                  x1_ref, retn_ref, convn_ref,
                  z_ref, s_ref, ext_ref, ycat_ref, *, tm, chunk, sdecay):
    i = pl.program_id(1)
    last = pl.num_programs(1) - 1

    @pl.when(i == 0)
    def _():
        s_ref[...] = ret0_ref[...]
        ext_ref[HIST_BASE - CONV_HIST:HIST_BASE, :] = conv0_ref[...]

    x = x_ref[...]
    h = _rms(x, nmix_ref[...])
    z_ref[...] = jnp.dot(h.astype(BF16), win_ref[...], preferred_element_type=F32)

    for c in range(tm // chunk):
        rows = slice(c * chunk, (c + 1) * chunk)
        cq, sq, ck, sk = cq_ref[rows, :], sq_ref[rows, :], ck_ref[rows, :], sk_ref[rows, :]
        for hd in range(RET_HEADS):
            lo = hd * RET_HEAD_DIM
            q = z_ref[rows, lo:lo + RET_HEAD_DIM]
            k = z_ref[rows, RET_WIDTH + lo:RET_WIDTH + lo + RET_HEAD_DIM]
            v = z_ref[rows, 2 * RET_WIDTH + lo:2 * RET_WIDTH + lo + RET_HEAD_DIM]
            g = z_ref[rows, 3 * RET_WIDTH + lo:3 * RET_WIDTH + lo + RET_HEAD_DIM]
            qr = q * cq + pltpu.roll(q, RET_HEAD_DIM // 2, 1) * sq
            kr = k * ck + pltpu.roll(k, RET_HEAD_DIM // 2, 1) * sk
            qb, vb = qr.astype(BF16), v.astype(BF16)
            sc = lax.dot_general(qb, kr.astype(BF16), NT_DIMS, preferred_element_type=F32) * dmask_ref[hd]
            s_prev = s_ref[hd]
            o = (jnp.dot(sc.astype(BF16), vb, preferred_element_type=F32)
                 + jnp.dot(qb, s_prev.astype(BF16), preferred_element_type=F32) * cross_ref[hd])
            kw = (kr * kdec_ref[hd]).astype(BF16)
            s_ref[hd] = sdecay[hd] * s_prev + lax.dot_general(kw, vb, TN_DIMS, preferred_element_type=F32)
            on = _center_norm(o) * gn_ref[:, lo:lo + RET_HEAD_DIM]
            ycat_ref[rows, lo:lo + RET_HEAD_DIM] = (g * jax.nn.sigmoid(g) * on).astype(BF16)

    a = z_ref[:, 4 * RET_WIDTH:4 * RET_WIDTH + CONV_WIDTH]
    b = z_ref[:, 4 * RET_WIDTH + CONV_WIDTH:IN_WIDTH]
    ext_ref[HIST_BASE:HIST_BASE + tm, :] = a * jax.nn.sigmoid(b)
    first = HIST_BASE - CONV_HIST
    for r in range(tm // CONV_ROWS):
        acc = jnp.broadcast_to(cb_ref[...], (CONV_ROWS, CONV_WIDTH))
        for j in range(CONV_KERNEL):
            start = first + r * CONV_ROWS + j
            acc = acc + cw_ref[j:j + 1, :] * ext_ref[start:start + CONV_ROWS, :]
        cn = _center_norm(acc) * lng_ref[...] + lnb_ref[...]
        ycat_ref[r * CONV_ROWS:(r + 1) * CONV_ROWS, RET_WIDTH:] = (cn * jax.nn.sigmoid(cn)).astype(BF16)
    hist = ext_ref[first + tm:HIST_BASE + tm, :]
    ext_ref[first:HIST_BASE, :] = hist

    @pl.when(i == last)
    def _():
        convn_ref[...] = hist
        retn_ref[...] = s_ref[...]

    x1_ref[...] = x + jnp.dot(ycat_ref[...], wout_ref[...], preferred_element_type=F32)


def _const_spec(shape):
    return pl.BlockSpec(shape, lambda b, i: (0,) * len(shape))


def _mixer(x, ret0, conv0, pos0, w, *, tm, chunk):
    bsz, seq, _ = x.shape
    assert seq % tm == 0 and tm % chunk == 0 and tm % CONV_ROWS == 0 and tm >= CONV_HIST
    half = RET_HEAD_DIM // 2
    inv_freq = ROPE_BASE ** (-jnp.arange(half, dtype=F32) / half)
    ang = (pos0 + jnp.arange(seq, dtype=F32))[:, None] * inv_freq[None, :]
    cos, sin = jnp.cos(ang), jnp.sin(ang)
    cos2 = jnp.concatenate([cos, cos], axis=-1)
    sin2 = jnp.concatenate([-sin, sin], axis=-1)
    kscale = RET_HEAD_DIM ** -0.5
    log_gamma = jnp.log1p(-jnp.exp2(-5.0 - jnp.arange(RET_HEADS, dtype=F32)))
    idx = jnp.arange(chunk, dtype=F32)
    diff = idx[:, None] - idx[None, :]
    causal = diff >= 0
    dmask = jnp.where(causal[None], jnp.exp(log_gamma[:, None, None] * jnp.where(causal, diff, 0.0)[None]), 0.0)
    cross = jnp.broadcast_to(jnp.exp(log_gamma[:, None] * (idx[None, :] + 1.0))[:, :, None],
                             (RET_HEADS, chunk, RET_HEAD_DIM))
    kdec = jnp.broadcast_to(jnp.exp(log_gamma[:, None] * (chunk - 1.0 - idx[None, :]))[:, :, None],
                            (RET_HEADS, chunk, RET_HEAD_DIM))
    lg64 = np.log1p(-np.exp2(-5.0 - np.arange(RET_HEADS, dtype=np.float64)))
    sdecay = tuple(float(np.float32(np.exp(lg64[h] * chunk))) for h in range(RET_HEADS))

    tok_spec = pl.BlockSpec((None, tm, D_MODEL), lambda b, i: (b, i, 0))
    tab_spec = pl.BlockSpec((tm, RET_HEAD_DIM), lambda b, i: (i, 0))
    ret_spec = pl.BlockSpec((None, RET_HEADS, RET_HEAD_DIM, RET_HEAD_DIM), lambda b, i: (b, 0, 0, 0))
    conv_spec = pl.BlockSpec((None, CONV_HIST, CONV_WIDTH), lambda b, i: (b, 0, 0))
    return pl.pallas_call(
        functools.partial(_mixer_kernel, tm=tm, chunk=chunk, sdecay=sdecay),
        grid=(bsz, seq // tm),
        in_specs=[tok_spec, ret_spec, conv_spec,
                  _const_spec((1, D_MODEL)), _const_spec((D_MODEL, IN_WIDTH)),
                  tab_spec, tab_spec, tab_spec, tab_spec,
                  _const_spec((RET_HEADS, chunk, chunk)),
                  _const_spec((RET_HEADS, chunk, RET_HEAD_DIM)), _const_spec((RET_HEADS, chunk, RET_HEAD_DIM)),
                  _const_spec((1, RET_WIDTH)), _const_spec((CONV_KERNEL, CONV_WIDTH)),
                  _const_spec((1, CONV_WIDTH)), _const_spec((1, CONV_WIDTH)), _const_spec((1, CONV_WIDTH)),
                  _const_spec((D_MODEL, D_MODEL))],
        out_specs=[tok_spec, ret_spec, conv_spec],
        out_shape=[jax.ShapeDtypeStruct((bsz, seq, D_MODEL), F32),
                   jax.ShapeDtypeStruct((bsz, RET_HEADS, RET_HEAD_DIM, RET_HEAD_DIM), F32),
                   jax.ShapeDtypeStruct((bsz, CONV_HIST, CONV_WIDTH), F32)],
        scratch_shapes=[pltpu.VMEM((tm, IN_WIDTH), F32),
                        pltpu.VMEM((RET_HEADS, RET_HEAD_DIM, RET_HEAD_DIM), F32),
                        pltpu.VMEM((HIST_BASE + tm, CONV_WIDTH), F32),
                        pltpu.VMEM((tm, D_MODEL), BF16)],
        compiler_params=pltpu.CompilerParams(dimension_semantics=("arbitrary", "arbitrary"),
                                             vmem_limit_bytes=VMEM_LIMIT),
        name="mixer",
    )(x, ret0, conv0, w["norm_mix"], w["w_in"], cos2, sin2, cos2 * kscale, sin2 * kscale,
      dmask, cross, kdec, w["ret_gn_gain"], w["conv_dw"], w["conv_dw_bias"], w["conv_ln_gain"],
      w["conv_ln_bias"], w["w_out"])


def _top16_distinct(sc, want_rank):
    s = sc
    rank = jnp.full(sc.shape, float(PEER_TOPK), F32)
    tops = []
    for k in range(PEER_TOPK):
        m = jnp.max(s, axis=0, keepdims=True)
        hit = s == m
        s = jnp.where(hit, -jnp.inf, s)
        if want_rank:
            rank = jnp.where(hit, float(k), rank)
        tops.append(m)
    removed = jnp.sum(jnp.where(s == -jnp.inf, 1.0, 0.0), axis=0, keepdims=True)
    return jnp.concatenate(tops, axis=0), removed, rank


def _top16_ranked(sc, iota):
    s = sc
    rank = jnp.full(sc.shape, float(PEER_TOPK), F32)
    tops = []
    for k in range(PEER_TOPK):
        m = jnp.max(s, axis=0, keepdims=True)
        idx = jnp.min(jnp.where(s == m, iota, float(PEER_KEYS)), axis=0, keepdims=True)
        hit = iota == idx
        s = jnp.where(hit, -jnp.inf, s)
        rank = jnp.where(hit, float(k), rank)
        tops.append(m)
    return jnp.concatenate(tops, axis=0), rank


def _merge16(a, b, iota16):
    count = jnp.zeros_like(a)
    front = a + b[0:1, :]
    best0 = None
    norm = None
    for k in range(PEER_TOPK):
        m = jnp.max(front, axis=0, keepdims=True)
        isel = jnp.min(jnp.where(front == m, iota16, float(PEER_TOPK)), axis=0, keepdims=True)
        hit = iota16 == isel
        if k == 0:
            best0 = m
            norm = jnp.ones_like(m)
        else:
            norm = norm + jnp.exp(m - best0)
        count = count + jnp.where(hit, 1.0, 0.0)
        if k + 1 < PEER_TOPK:
            nxt = jnp.sum(jnp.where(hit, count, 0.0), axis=0, keepdims=True)
            b_nxt = jnp.sum(jnp.where(iota16 == nxt, b, 0.0), axis=0, keepdims=True)
            b_nxt = jnp.where(nxt >= float(PEER_TOPK), -jnp.inf, b_nxt)
            front = jnp.where(hit, a + b_nxt, front)
    return count, norm


def _route(qT_ref, keys_ref, ra_ref, ea_ref, rb_ref, eb_ref, *, tm):
    iota16 = lax.broadcasted_iota(jnp.int32, (PEER_TOPK, LANES), 0).astype(F32)
    blocks = [slice(sb * LANES, (sb + 1) * LANES) for sb in range(tm // LANES)]

    def scores(hd, half, lanes):
        row0 = pl.multiple_of((hd * 2 + half) * PEER_KEY_DIM, PEER_KEY_DIM)
        qs = qT_ref[pl.ds(row0, PEER_KEY_DIM), lanes].astype(BF16)
        return jnp.dot(keys_ref[hd * 2 + half], qs, preferred_element_type=F32)

    def write_tables(hd, lanes, sc_a, sc_b, top_a, top_b, taken, rank_b, norm):
        tiled = ra_ref.shape[1:3]
        ra_ref[hd, :, :, lanes] = taken.reshape(*tiled, LANES)
        ea_ref[hd, :, :, lanes] = (jnp.exp(sc_a - top_a[0:1, :]) / norm).reshape(*tiled, LANES)
        rb_ref[hd, :, lanes] = rank_b.astype(BF16)
        eb_ref[hd, :, lanes] = jnp.exp(sc_b - top_b[0:1, :]).astype(BF16)

    def distinct_body(hd, removed):
        for lanes in blocks:
            sc_a, sc_b = scores(hd, 0, lanes), scores(hd, 1, lanes)
            top_a, removed_a, _ = _top16_distinct(sc_a, False)
            top_b, removed_b, rank_b = _top16_distinct(sc_b, True)
            count, norm = _merge16(top_a, top_b, iota16)
            taken = jnp.zeros(sc_a.shape, F32)
            for k in range(PEER_TOPK):
                taken = jnp.where(sc_a == top_a[k:k + 1, :], count[k:k + 1, :], taken)
            write_tables(hd, lanes, sc_a, sc_b, top_a, top_b, taken, rank_b, norm)
            removed = jnp.maximum(removed, jnp.maximum(removed_a, removed_b))
        return removed

    removed = lax.fori_loop(0, PEER_HEADS, distinct_body, jnp.zeros((1, LANES), F32))

    @pl.when(jnp.max(removed) > float(PEER_TOPK))
    def _():
        iota = lax.broadcasted_iota(jnp.int32, (PEER_KEYS, LANES), 0).astype(F32)

        def ranked_body(hd, carry):
            for lanes in blocks:
                sc_a, sc_b = scores(hd, 0, lanes), scores(hd, 1, lanes)
                top_a, rank_a = _top16_ranked(sc_a, iota)
                top_b, rank_b = _top16_ranked(sc_b, iota)
                count, norm = _merge16(top_a, top_b, iota16)
                taken = jnp.zeros(sc_a.shape, F32)
                for k in range(PEER_TOPK):
                    taken = jnp.where(rank_a == float(k), count[k:k + 1, :], taken)
                write_tables(hd, lanes, sc_a, sc_b, top_a, top_b, taken, rank_b, norm)
            return carry

        lax.fori_loop(0, PEER_HEADS, ranked_body, 0)


def _ffn_kernel(x1_ref, nffn_ref, wqT_ref, keys_ref, down_ref, upT_ref, nfin_ref, y_ref,
                h2_ref, ra_ref, ea_ref, rb_ref, eb_ref, act_ref, wT_ref, acc_ref, *, tm, te):
    j = pl.program_id(1)
    n_first = te // PEER_KEYS

    @pl.when(j == 0)
    def _():
        h2 = _rms(x1_ref[...], nffn_ref[...]).astype(BF16)
        h2_ref[...] = h2
        act_ref[0:N_QUERY, :] = lax.dot_general(wqT_ref[...], h2, NT_DIMS, preferred_element_type=F32)
        _route(act_ref, keys_ref, ra_ref, ea_ref, rb_ref, eb_ref, tm=tm)
        acc_ref[...] = jnp.zeros_like(acc_ref)

    h2 = h2_ref[...]
    grp_rows = GROUP_FIRST * PEER_KEYS
    groups = [slice(g * grp_rows, (g + 1) * grp_rows) for g in range(te // grp_rows)]
    for rows in groups:
        act_ref[rows, :] = lax.dot_general(down_ref[rows, :], h2, NT_DIMS, preferred_element_type=F32)
    for g, rows in enumerate(groups):
        for sb in range(tm // LANES):
            lanes = slice(sb * LANES, (sb + 1) * LANES)
            for a in range(GROUP_FIRST):
                r = g * GROUP_FIRST + a
                thr = [jnp.broadcast_to(ra_ref[hd, j, r:r + 1, lanes], (BF16_ROWS, LANES)).astype(BF16)
                       for hd in range(PEER_HEADS)]
                gat = [jnp.broadcast_to(ea_ref[hd, j, r:r + 1, lanes], (BF16_ROWS, LANES)).astype(BF16)
                       for hd in range(PEER_HEADS)]
                for slab in range(PEER_KEYS // BF16_ROWS):
                    keys = slice(slab * BF16_ROWS, (slab + 1) * BF16_ROWS)
                    gate = jnp.zeros((BF16_ROWS, LANES), BF16)
                    for hd in range(PEER_HEADS):
                        hit = rb_ref[hd, keys, lanes] < thr[hd]
                        gate = gate + jnp.where(hit, eb_ref[hd, keys, lanes], 0.0) * gat[hd]
                    row0 = rows.start + a * PEER_KEYS + slab * BF16_ROWS
                    act = _gelu_tanh(act_ref[row0:row0 + BF16_ROWS, lanes])
                    wT_ref[row0:row0 + BF16_ROWS, lanes] = gate * act.astype(BF16)
        acc_ref[...] += jnp.dot(upT_ref[:, rows], wT_ref[rows, :], preferred_element_type=F32)

    @pl.when(j == pl.num_programs(1) - 1)
    def _():
        y_ref[...] = _rms(x1_ref[...] + acc_ref[...].T, nfin_ref[...])


def _ffn(x1, w, *, tm, te):
    shape = x1.shape
    x1 = x1.reshape(-1, D_MODEL)
    t = x1.shape[0]
    assert t % tm == 0 and tm % LANES == 0 and PEER_EXPERTS % te == 0
    assert te % (SUBLANES * PEER_KEYS) == 0 and SUBLANES % GROUP_FIRST == 0
    tok_spec = pl.BlockSpec((tm, D_MODEL), lambda i, j: (i, 0))
    tab = (PEER_HEADS, PEER_KEYS, tm)
    tab_first = (PEER_HEADS, PEER_EXPERTS // te, te // PEER_KEYS, tm)
    return pl.pallas_call(
        functools.partial(_ffn_kernel, tm=tm, te=te),
        grid=(t // tm, PEER_EXPERTS // te),
        in_specs=[tok_spec,
                  pl.BlockSpec((1, D_MODEL), lambda i, j: (0, 0)),
                  pl.BlockSpec((N_QUERY, D_MODEL), lambda i, j: (0, 0)),
                  pl.BlockSpec((PEER_HEADS * 2, PEER_KEYS, PEER_KEY_DIM), lambda i, j: (0, 0, 0)),
                  pl.BlockSpec((te, D_MODEL), lambda i, j: (j, 0)),
                  pl.BlockSpec((D_MODEL, te), lambda i, j: (0, j)),
                  pl.BlockSpec((1, D_MODEL), lambda i, j: (0, 0))],
        out_specs=tok_spec,
        out_shape=jax.ShapeDtypeStruct((t, D_MODEL), F32),
        scratch_shapes=[pltpu.VMEM((tm, D_MODEL), BF16),
                        pltpu.VMEM(tab_first, F32), pltpu.VMEM(tab_first, F32),
                        pltpu.VMEM(tab, BF16), pltpu.VMEM(tab, BF16),
                        pltpu.VMEM((max(te, N_QUERY), tm), F32), pltpu.VMEM((te, tm), BF16),
                        pltpu.VMEM((D_MODEL, tm), F32)],
        compiler_params=pltpu.CompilerParams(dimension_semantics=("arbitrary", "arbitrary"),
                                             vmem_limit_bytes=VMEM_LIMIT),
        name="ffn",
    )(x1, w["norm_ffn"], w["peer_query_t"], w["peer_keys"], w["peer_down"], w["peer_up_t"],
      w["norm_final"]).reshape(shape)


def kernel(x_prompt, x_sample, state_retention, state_conv, norm_mix, w_in, ret_gn_gain, conv_dw, conv_dw_bias,
           conv_ln_gain, conv_ln_bias, w_out, norm_ffn, peer_query, peer_sub_keys, peer_down, peer_up, norm_final):
    assert w_in.shape[0] == 1, "single-layer model"
    row = lambda v: v.reshape(1, -1).astype(F32)
    w = {
        "norm_mix": row(norm_mix[0]), "w_in": w_in[0].astype(BF16), "ret_gn_gain": row(ret_gn_gain[0]),
        "conv_dw": conv_dw[0].astype(F32), "conv_dw_bias": row(conv_dw_bias[0]),
        "conv_ln_gain": row(conv_ln_gain[0]), "conv_ln_bias": row(conv_ln_bias[0]),
        "w_out": w_out[0].astype(BF16), "norm_ffn": row(norm_ffn[0]),
        "peer_query_t": peer_query[0].T.astype(BF16),
        "peer_keys": peer_sub_keys[0].reshape(PEER_HEADS * 2, PEER_KEYS, PEER_KEY_DIM).astype(BF16),
        "peer_down": peer_down[0].astype(BF16), "peer_up_t": peer_up[0].T.astype(BF16),
        "norm_final": row(norm_final),
    }
    bp = x_prompt.shape[0]
    ret0 = jnp.zeros((bp, RET_HEADS, RET_HEAD_DIM, RET_HEAD_DIM), F32)
    conv0 = jnp.zeros((bp, CONV_HIST, CONV_WIDTH), F32)
    x1p, ret_p, conv_p = _mixer(x_prompt, ret0, conv0, 0.0, w, tm=512, chunk=256)
    x1s, ret_s, conv_s = _mixer(x_sample, state_retention[0], state_conv[0], float(PAST_LEN), w,
                                tm=x_sample.shape[1], chunk=x_sample.shape[1])
    y_prompt = _ffn(x1p, w, tm=512, te=2048)
    y_sample = _ffn(x1s, w, tm=256, te=2048)
    return (y_prompt, y_sample, ret_p[None], ret_s[None], conv_p[None], conv_s[None])
```

```python
import functools

import numpy as np
import jax
import jax.numpy as jnp
from jax import lax
from jax.experimental import pallas as pl
from jax.experimental.pallas import tpu as pltpu

D_MODEL = 1024
RET_HEADS = 4
RET_HEAD_DIM = 128
RET_WIDTH = RET_HEADS * RET_HEAD_DIM
CONV_WIDTH = D_MODEL - RET_WIDTH
IN_WIDTH = 4 * RET_WIDTH + 2 * CONV_WIDTH
CONV_KERNEL = 31
CONV_HIST = CONV_KERNEL - 1
ROPE_BASE = 10000.0
PEER_HEADS = 8
PEER_KEYS = 128
PEER_TOPK = 16
PEER_KEY_DIM = 128
PEER_EXPERTS = PEER_KEYS * PEER_KEYS
EPS = 1e-6
PAST_LEN = 2048

LANES = 128
SUBLANES = 8
BF16_ROWS = 16
GROUP_FIRST = 4
N_QUERY = PEER_HEADS * 2 * PEER_KEY_DIM
HIST_BASE = 32
CONV_ROWS = 32
VMEM_LIMIT = 56 * 1024 * 1024

F32 = jnp.float32
BF16 = jnp.bfloat16
NT_DIMS = (((1,), (1,)), ((), ()))
TN_DIMS = (((0,), (0,)), ((), ()))


def _rms(x, gain):
    return x * lax.rsqrt(jnp.mean(x * x, axis=-1, keepdims=True) + EPS) * gain


def _center_norm(x):
    xc = x - jnp.mean(x, axis=-1, keepdims=True)
    return xc * lax.rsqrt(jnp.mean(xc * xc, axis=-1, keepdims=True) + EPS)


def _gelu_tanh(x):
    k1 = float(np.sqrt(2.0 / np.pi))
    k2 = k1 * 0.044715
    return (0.5 * x) * (1.0 + jnp.tanh(x * (k1 + k2 * (x * x))))


def _mixer_kernel(x_ref, ret0_ref, conv0_ref, nmix_ref, win_ref, cq_ref, sq_ref, ck_ref, sk_ref,
                  dmask_ref, cross_ref, kdec_ref, gn_ref, cw_ref, cb_ref, lng_ref, lnb_ref, wout_ref,
                  x1_ref, retn_ref, convn_ref,
                  z_ref, s_ref, ext_ref, shift_ref, ycat_ref, *, tm, chunk, sdecay):
    i = pl.program_id(1)
    last = pl.num_programs(1) - 1

    @pl.when(i == 0)
    def _():
        s_ref[...] = ret0_ref[...]
        ext_ref[HIST_BASE - CONV_HIST:HIST_BASE, :] = conv0_ref[...]

    x = x_ref[...]
    h = _rms(x, nmix_ref[...])
    z_ref[...] = jnp.dot(h.astype(BF16), win_ref[...], preferred_element_type=F32)

    for c in range(tm // chunk):
        rows = slice(c * chunk, (c + 1) * chunk)
        cq, sq, ck, sk = cq_ref[rows, :], sq_ref[rows, :], ck_ref[rows, :], sk_ref[rows, :]
        for hd in range(RET_HEADS):
            lo = hd * RET_HEAD_DIM
            q = z_ref[rows, lo:lo + RET_HEAD_DIM]
            k = z_ref[rows, RET_WIDTH + lo:RET_WIDTH + lo + RET_HEAD_DIM]
            v = z_ref[rows, 2 * RET_WIDTH + lo:2 * RET_WIDTH + lo + RET_HEAD_DIM]
            g = z_ref[rows, 3 * RET_WIDTH + lo:3 * RET_WIDTH + lo + RET_HEAD_DIM]
            qr = q * cq + pltpu.roll(q, RET_HEAD_DIM // 2, 1) * sq
            kr = k * ck + pltpu.roll(k, RET_HEAD_DIM // 2, 1) * sk
            qb, vb = qr.astype(BF16), v.astype(BF16)
            sc = lax.dot_general(qb, kr.astype(BF16), NT_DIMS, preferred_element_type=F32) * dmask_ref[hd]
            s_prev = s_ref[hd]
            o = (jnp.dot(sc.astype(BF16), vb, preferred_element_type=F32)
                 + jnp.dot(qb, s_prev.astype(BF16), preferred_element_type=F32) * cross_ref[hd])
            kw = (kr * kdec_ref[hd]).astype(BF16)
            s_ref[hd] = sdecay[hd] * s_prev + lax.dot_general(kw, vb, TN_DIMS, preferred_element_type=F32)
            on = _center_norm(o) * gn_ref[:, lo:lo + RET_HEAD_DIM]
            ycat_ref[rows, lo:lo + RET_HEAD_DIM] = (g * jax.nn.sigmoid(g) * on).astype(BF16)

    a = z_ref[:, 4 * RET_WIDTH:4 * RET_WIDTH + CONV_WIDTH]
    b = z_ref[:, 4 * RET_WIDTH + CONV_WIDTH:IN_WIDTH]
    ext_ref[HIST_BASE:HIST_BASE + tm, :] = a * jax.nn.sigmoid(b)
    first = HIST_BASE - CONV_HIST
    taps = []
    for shift in range(SUBLANES):
        js = [j for j in range(CONV_KERNEL) if (first + j) % SUBLANES == shift]
        bases = [first + j - shift for j in js]
        if shift == 0:
            src = ext_ref
        else:
            src = shift_ref.at[shift - 1]
            lo, hi = min(bases), max(bases) + tm
            src[lo:hi, :] = ext_ref[lo + shift:hi + shift, :]
        taps += [(j, src, base) for j, base in zip(js, bases)]
    for r in range(tm // CONV_ROWS):
        acc = jnp.broadcast_to(cb_ref[...], (CONV_ROWS, CONV_WIDTH))
        for j, src, base in taps:
            start = base + r * CONV_ROWS
            acc = acc + cw_ref[j:j + 1, :] * src[start:start + CONV_ROWS, :]
        cn = _center_norm(acc) * lng_ref[...] + lnb_ref[...]
        ycat_ref[r * CONV_ROWS:(r + 1) * CONV_ROWS, RET_WIDTH:] = (cn * jax.nn.sigmoid(cn)).astype(BF16)
    hist = ext_ref[first + tm:HIST_BASE + tm, :]
    ext_ref[first:HIST_BASE, :] = hist

    @pl.when(i == last)
    def _():
        convn_ref[...] = hist
        retn_ref[...] = s_ref[...]

    x1_ref[...] = x + jnp.dot(ycat_ref[...], wout_ref[...], preferred_element_type=F32)


def _const_spec(shape):
    return pl.BlockSpec(shape, lambda b, i: (0,) * len(shape))


def _mixer(x, ret0, conv0, pos0, w, *, tm, chunk):
    bsz, seq, _ = x.shape
    assert seq % tm == 0 and tm % chunk == 0 and tm % CONV_ROWS == 0 and tm >= CONV_HIST
    half = RET_HEAD_DIM // 2
    inv_freq = ROPE_BASE ** (-jnp.arange(half, dtype=F32) / half)
    ang = (pos0 + jnp.arange(seq, dtype=F32))[:, None] * inv_freq[None, :]
    cos, sin = jnp.cos(ang), jnp.sin(ang)
    cos2 = jnp.concatenate([cos, cos], axis=-1)
    sin2 = jnp.concatenate([-sin, sin], axis=-1)
    kscale = RET_HEAD_DIM ** -0.5
    log_gamma = jnp.log1p(-jnp.exp2(-5.0 - jnp.arange(RET_HEADS, dtype=F32)))
    idx = jnp.arange(chunk, dtype=F32)
    diff = idx[:, None] - idx[None, :]
    causal = diff >= 0
    dmask = jnp.where(causal[None], jnp.exp(log_gamma[:, None, None] * jnp.where(causal, diff, 0.0)[None]), 0.0)
    cross = jnp.broadcast_to(jnp.exp(log_gamma[:, None] * (idx[None, :] + 1.0))[:, :, None],
                             (RET_HEADS, chunk, RET_HEAD_DIM))
    kdec = jnp.broadcast_to(jnp.exp(log_gamma[:, None] * (chunk - 1.0 - idx[None, :]))[:, :, None],
                            (RET_HEADS, chunk, RET_HEAD_DIM))
    lg64 = np.log1p(-np.exp2(-5.0 - np.arange(RET_HEADS, dtype=np.float64)))
    sdecay = tuple(float(np.float32(np.exp(lg64[h] * chunk))) for h in range(RET_HEADS))

    tok_spec = pl.BlockSpec((None, tm, D_MODEL), lambda b, i: (b, i, 0))
    tab_spec = pl.BlockSpec((tm, RET_HEAD_DIM), lambda b, i: (i, 0))
    ret_spec = pl.BlockSpec((None, RET_HEADS, RET_HEAD_DIM, RET_HEAD_DIM), lambda b, i: (b, 0, 0, 0))
    conv_spec = pl.BlockSpec((None, CONV_HIST, CONV_WIDTH), lambda b, i: (b, 0, 0))
    return pl.pallas_call(
        functools.partial(_mixer_kernel, tm=tm, chunk=chunk, sdecay=sdecay),
        grid=(bsz, seq // tm),
        in_specs=[tok_spec, ret_spec, conv_spec,
                  _const_spec((1, D_MODEL)), _const_spec((D_MODEL, IN_WIDTH)),
                  tab_spec, tab_spec, tab_spec, tab_spec,
                  _const_spec((RET_HEADS, chunk, chunk)),
                  _const_spec((RET_HEADS, chunk, RET_HEAD_DIM)), _const_spec((RET_HEADS, chunk, RET_HEAD_DIM)),
                  _const_spec((1, RET_WIDTH)), _const_spec((CONV_KERNEL, CONV_WIDTH)),
                  _const_spec((1, CONV_WIDTH)), _const_spec((1, CONV_WIDTH)), _const_spec((1, CONV_WIDTH)),
                  _const_spec((D_MODEL, D_MODEL))],
        out_specs=[tok_spec, ret_spec, conv_spec],
        out_shape=[jax.ShapeDtypeStruct((bsz, seq, D_MODEL), F32),
                   jax.ShapeDtypeStruct((bsz, RET_HEADS, RET_HEAD_DIM, RET_HEAD_DIM), F32),
                   jax.ShapeDtypeStruct((bsz, CONV_HIST, CONV_WIDTH), F32)],
        scratch_shapes=[pltpu.VMEM((tm, IN_WIDTH), F32),
                        pltpu.VMEM((RET_HEADS, RET_HEAD_DIM, RET_HEAD_DIM), F32),
                        pltpu.VMEM((HIST_BASE + tm, CONV_WIDTH), F32),
                        pltpu.VMEM((SUBLANES - 1, HIST_BASE + tm, CONV_WIDTH), F32),
                        pltpu.VMEM((tm, D_MODEL), BF16)],
        compiler_params=pltpu.CompilerParams(dimension_semantics=("arbitrary", "arbitrary"),
                                             vmem_limit_bytes=VMEM_LIMIT),
        name="mixer",
    )(x, ret0, conv0, w["norm_mix"], w["w_in"], cos2, sin2, cos2 * kscale, sin2 * kscale,
      dmask, cross, kdec, w["ret_gn_gain"], w["conv_dw"], w["conv_dw_bias"], w["conv_ln_gain"],
      w["conv_ln_bias"], w["w_out"])


def _top16_distinct(sc, want_rank):
    s = sc
    rank = jnp.full(sc.shape, float(PEER_TOPK), F32)
    tops = []
    for k in range(PEER_TOPK):
        m = jnp.max(s, axis=0, keepdims=True)
        hit = s == m
        s = jnp.where(hit, -jnp.inf, s)
        if want_rank:
            rank = jnp.where(hit, float(k), rank)
        tops.append(m)
    removed = jnp.sum(jnp.where(s == -jnp.inf, 1.0, 0.0), axis=0, keepdims=True)
    return jnp.concatenate(tops, axis=0), removed, rank


def _top16_ranked(sc, iota):
    s = sc
    rank = jnp.full(sc.shape, float(PEER_TOPK), F32)
    tops = []
    for k in range(PEER_TOPK):
        m = jnp.max(s, axis=0, keepdims=True)
        idx = jnp.min(jnp.where(s == m, iota, float(PEER_KEYS)), axis=0, keepdims=True)
        hit = iota == idx
        s = jnp.where(hit, -jnp.inf, s)
        rank = jnp.where(hit, float(k), rank)
        tops.append(m)
    return jnp.concatenate(tops, axis=0), rank


def _merge16(a, b, iota16):
    count = jnp.zeros_like(a)
    front = a + b[0:1, :]
    best0 = None
    norm = None
    for k in range(PEER_TOPK):
        m = jnp.max(front, axis=0, keepdims=True)
        isel = jnp.min(jnp.where(front == m, iota16, float(PEER_TOPK)), axis=0, keepdims=True)
        hit = iota16 == isel
        if k == 0:
            best0 = m
            norm = jnp.ones_like(m)
        else:
            norm = norm + jnp.exp(m - best0)
        count = count + jnp.where(hit, 1.0, 0.0)
        if k + 1 < PEER_TOPK:
            nxt = jnp.sum(jnp.where(hit, count, 0.0), axis=0, keepdims=True)
            b_nxt = jnp.sum(jnp.where(iota16 == nxt, b, 0.0), axis=0, keepdims=True)
            b_nxt = jnp.where(nxt >= float(PEER_TOPK), -jnp.inf, b_nxt)
            front = jnp.where(hit, a + b_nxt, front)
    return count, norm


def _route(qT_ref, keys_ref, ra_ref, ea_ref, rb_ref, eb_ref, *, tm):
    iota16 = lax.broadcasted_iota(jnp.int32, (PEER_TOPK, LANES), 0).astype(F32)
    blocks = [slice(sb * LANES, (sb + 1) * LANES) for sb in range(tm // LANES)]

    def scores(hd, half, lanes):
        row0 = pl.multiple_of((hd * 2 + half) * PEER_KEY_DIM, PEER_KEY_DIM)
        qs = qT_ref[pl.ds(row0, PEER_KEY_DIM), lanes].astype(BF16)
        return jnp.dot(keys_ref[hd * 2 + half], qs, preferred_element_type=F32)

    def write_tables(hd, lanes, sc_a, sc_b, top_a, top_b, taken, rank_b, norm):
        tiled = ra_ref.shape[1:3]
        ra_ref[hd, :, :, lanes] = taken.reshape(*tiled, LANES)
        ea_ref[hd, :, :, lanes] = (jnp.exp(sc_a - top_a[0:1, :]) / norm).reshape(*tiled, LANES)
        rb_ref[hd, :, lanes] = rank_b.astype(BF16)
        eb_ref[hd, :, lanes] = jnp.exp(sc_b - top_b[0:1, :]).astype(BF16)

    def ranked_unit(hd, lanes):
        iota = lax.broadcasted_iota(jnp.int32, (PEER_KEYS, LANES), 0).astype(F32)
        sc_a, sc_b = scores(hd, 0, lanes), scores(hd, 1, lanes)
        top_a, rank_a = _top16_ranked(sc_a, iota)
        top_b, rank_b = _top16_ranked(sc_b, iota)
        count, norm = _merge16(top_a, top_b, iota16)
        taken = jnp.zeros(sc_a.shape, F32)
        for k in range(PEER_TOPK):
            taken = jnp.where(rank_a == float(k), count[k:k + 1, :], taken)
        write_tables(hd, lanes, sc_a, sc_b, top_a, top_b, taken, rank_b, norm)

    def head_body(hd, carry):
        tied = []
        for lanes in blocks:
            sc_a, sc_b = scores(hd, 0, lanes), scores(hd, 1, lanes)
            top_a, removed_a, _ = _top16_distinct(sc_a, False)
            top_b, removed_b, rank_b = _top16_distinct(sc_b, True)
            count, norm = _merge16(top_a, top_b, iota16)
            taken = jnp.zeros(sc_a.shape, F32)
            for k in range(PEER_TOPK):
                taken = jnp.where(sc_a == top_a[k:k + 1, :], count[k:k + 1, :], taken)
            write_tables(hd, lanes, sc_a, sc_b, top_a, top_b, taken, rank_b, norm)
            tied.append(jnp.max(jnp.maximum(removed_a, removed_b)) > float(PEER_TOPK))
        for lanes, tie in zip(blocks, tied):
            pl.when(tie)(functools.partial(ranked_unit, hd, lanes))
        return carry

    lax.fori_loop(0, PEER_HEADS, head_body, 0)


def _ffn_kernel(x1_ref, nffn_ref, wqT_ref, keys_ref, down_ref, upT_ref, nfin_ref, y_ref,
                h2_ref, ra_ref, ea_ref, rb_ref, eb_ref, act_ref, wT_ref, acc_ref, *, tm, te):
    j = pl.program_id(1)
    n_first = te // PEER_KEYS

    @pl.when(j == 0)
    def _():
        h2 = _rms(x1_ref[...], nffn_ref[...]).astype(BF16)
        h2_ref[...] = h2
        act_ref[0:N_QUERY, :] = lax.dot_general(wqT_ref[...], h2, NT_DIMS, preferred_element_type=F32)
        _route(act_ref, keys_ref, ra_ref, ea_ref, rb_ref, eb_ref, tm=tm)
        acc_ref[...] = jnp.zeros_like(acc_ref)

    h2 = h2_ref[...]
    grp_rows = GROUP_FIRST * PEER_KEYS
    groups = [slice(g * grp_rows, (g + 1) * grp_rows) for g in range(te // grp_rows)]
    for rows in groups:
        act_ref[rows, :] = lax.dot_general(down_ref[rows, :], h2, NT_DIMS, preferred_element_type=F32)
    for g, rows in enumerate(groups):
        for sb in range(tm // LANES):
            lanes = slice(sb * LANES, (sb + 1) * LANES)
            for a in range(GROUP_FIRST):
                r = g * GROUP_FIRST + a
                thr = [jnp.broadcast_to(ra_ref[hd, j, r:r + 1, lanes], (BF16_ROWS, LANES)).astype(BF16)
                       for hd in range(PEER_HEADS)]
                gat = [jnp.broadcast_to(ea_ref[hd, j, r:r + 1, lanes], (BF16_ROWS, LANES)).astype(BF16)
                       for hd in range(PEER_HEADS)]
                for slab in range(PEER_KEYS // BF16_ROWS):
                    keys = slice(slab * BF16_ROWS, (slab + 1) * BF16_ROWS)
                    gate = jnp.zeros((BF16_ROWS, LANES), BF16)
                    for hd in range(PEER_HEADS):
                        hit = rb_ref[hd, keys, lanes] < thr[hd]
                        gate = gate + jnp.where(hit, eb_ref[hd, keys, lanes], 0.0) * gat[hd]
                    row0 = rows.start + a * PEER_KEYS + slab * BF16_ROWS
                    act = _gelu_tanh(act_ref[row0:row0 + BF16_ROWS, lanes])
                    wT_ref[row0:row0 + BF16_ROWS, lanes] = gate * act.astype(BF16)
        acc_ref[...] += jnp.dot(upT_ref[:, rows], wT_ref[rows, :], preferred_element_type=F32)

    @pl.when(j == pl.num_programs(1) - 1)
    def _():
        y_ref[...] = _rms(x1_ref[...] + acc_ref[...].T, nfin_ref[...])


def _ffn(x1, w, *, tm, te):
    shape = x1.shape
    x1 = x1.reshape(-1, D_MODEL)
    t = x1.shape[0]
    assert t % tm == 0 and tm % LANES == 0 and PEER_EXPERTS % te == 0
    assert te % (SUBLANES * PEER_KEYS) == 0 and SUBLANES % GROUP_FIRST == 0
    tok_spec = pl.BlockSpec((tm, D_MODEL), lambda i, j: (i, 0))
    tab = (PEER_HEADS, PEER_KEYS, tm)
    tab_first = (PEER_HEADS, PEER_EXPERTS // te, te // PEER_KEYS, tm)
    return pl.pallas_call(
        functools.partial(_ffn_kernel, tm=tm, te=te),
        grid=(t // tm, PEER_EXPERTS // te),
        in_specs=[tok_spec,
                  pl.BlockSpec((1, D_MODEL), lambda i, j: (0, 0)),
                  pl.BlockSpec((N_QUERY, D_MODEL), lambda i, j: (0, 0)),
                  pl.BlockSpec((PEER_HEADS * 2, PEER_KEYS, PEER_KEY_DIM), lambda i, j: (0, 0, 0)),
                  pl.BlockSpec((te, D_MODEL), lambda i, j: (j, 0)),
                  pl.BlockSpec((D_MODEL, te), lambda i, j: (0, j)),
                  pl.BlockSpec((1, D_MODEL), lambda i, j: (0, 0))],
        out_specs=tok_spec,
        out_shape=jax.ShapeDtypeStruct((t, D_MODEL), F32),
        scratch_shapes=[pltpu.VMEM((tm, D_MODEL), BF16),
                        pltpu.VMEM(tab_first, F32), pltpu.VMEM(tab_first, F32),
                        pltpu.VMEM(tab, BF16), pltpu.VMEM(tab, BF16),
                        pltpu.VMEM((max(te, N_QUERY), tm), F32), pltpu.VMEM((te, tm), BF16),
                        pltpu.VMEM((D_MODEL, tm), F32)],
        compiler_params=pltpu.CompilerParams(dimension_semantics=("arbitrary", "arbitrary"),
                                             vmem_limit_bytes=VMEM_LIMIT),
        name="ffn",
    )(x1, w["norm_ffn"], w["peer_query_t"], w["peer_keys"], w["peer_down"], w["peer_up_t"],
      w["norm_final"]).reshape(shape)


def kernel(x_prompt, x_sample, state_retention, state_conv, norm_mix, w_in, ret_gn_gain, conv_dw, conv_dw_bias,
           conv_ln_gain, conv_ln_bias, w_out, norm_ffn, peer_query, peer_sub_keys, peer_down, peer_up, norm_final):
    assert w_in.shape[0] == 1, "single-layer model"
    row = lambda v: v.reshape(1, -1).astype(F32)
    w = {
        "norm_mix": row(norm_mix[0]), "w_in": w_in[0].astype(BF16), "ret_gn_gain": row(ret_gn_gain[0]),
        "conv_dw": conv_dw[0].astype(F32), "conv_dw_bias": row(conv_dw_bias[0]),
        "conv_ln_gain": row(conv_ln_gain[0]), "conv_ln_bias": row(conv_ln_bias[0]),
        "w_out": w_out[0].astype(BF16), "norm_ffn": row(norm_ffn[0]),
        "peer_query_t": peer_query[0].T.astype(BF16),
        "peer_keys": peer_sub_keys[0].reshape(PEER_HEADS * 2, PEER_KEYS, PEER_KEY_DIM).astype(BF16),
        "peer_down": peer_down[0].astype(BF16), "peer_up_t": peer_up[0].T.astype(BF16),
        "norm_final": row(norm_final),
    }
    bp = x_prompt.shape[0]
    ret0 = jnp.zeros((bp, RET_HEADS, RET_HEAD_DIM, RET_HEAD_DIM), F32)
    conv0 = jnp.zeros((bp, CONV_HIST, CONV_WIDTH), F32)
    x1p, ret_p, conv_p = _mixer(x_prompt, ret0, conv0, 0.0, w, tm=512, chunk=256)
    x1s, ret_s, conv_s = _mixer(x_sample, state_retention[0], state_conv[0], float(PAST_LEN), w,
                                tm=x_sample.shape[1], chunk=x_sample.shape[1])
    y_prompt = _ffn(x1p, w, tm=512, te=2048)
    y_sample = _ffn(x1s, w, tm=256, te=2048)
    return (y_prompt, y_sample, ret_p[None], ret_s[None], conv_p[None], conv_s[None])
```

```python
import functools

import numpy as np
import jax
import jax.numpy as jnp
from jax import lax
from jax.experimental import pallas as pl
from jax.experimental.pallas import tpu as pltpu

D_MODEL = 1024
RET_HEADS = 4
RET_HEAD_DIM = 128
RET_WIDTH = RET_HEADS * RET_HEAD_DIM
CONV_WIDTH = D_MODEL - RET_WIDTH
IN_WIDTH = 4 * RET_WIDTH + 2 * CONV_WIDTH
CONV_KERNEL = 31
CONV_HIST = CONV_KERNEL - 1
ROPE_BASE = 10000.0
PEER_HEADS = 8
PEER_KEYS = 128
PEER_TOPK = 16
PEER_KEY_DIM = 128
PEER_EXPERTS = PEER_KEYS * PEER_KEYS
EPS = 1e-6
PAST_LEN = 2048

LANES = 128
SUBLANES = 8
BF16_ROWS = 16
GROUP_FIRST = 4
N_QUERY = PEER_HEADS * 2 * PEER_KEY_DIM
HIST_BASE = 32
CONV_ROWS = 32
VMEM_LIMIT = 56 * 1024 * 1024

F32 = jnp.float32
BF16 = jnp.bfloat16
NT_DIMS = (((1,), (1,)), ((), ()))
TN_DIMS = (((0,), (0,)), ((), ()))


def _rms(x, gain):
    return x * lax.rsqrt(jnp.mean(x * x, axis=-1, keepdims=True) + EPS) * gain


def _center_norm(x):
    xc = x - jnp.mean(x, axis=-1, keepdims=True)
    return xc * lax.rsqrt(jnp.mean(xc * xc, axis=-1, keepdims=True) + EPS)


def _gelu_tanh(x):
    k1 = float(np.sqrt(2.0 / np.pi))
    k2 = k1 * 0.044715
    return (0.5 * x) * (1.0 + jnp.tanh(x * (k1 + k2 * (x * x))))


def _mixer_kernel(x_ref, ret0_ref, conv0_ref, nmix_ref, win_ref, cq_ref, sq_ref, ck_ref, sk_ref,
                  dmask_ref, cross_ref, kdec_ref, gn_ref, cw_ref, cb_ref, lng_ref, lnb_ref, wout_ref,
                  x1_ref, retn_ref, convn_ref,
                  z_ref, s_ref, ext_ref, shift_ref, ycat_ref, *, tm, chunk, sdecay):
    i = pl.program_id(1)
    last = pl.num_programs(1) - 1

    @pl.when(i == 0)
    def _():
        s_ref[...] = ret0_ref[...]
        ext_ref[HIST_BASE - CONV_HIST:HIST_BASE, :] = conv0_ref[...]

    x = x_ref[...]
    h = _rms(x, nmix_ref[...])
    z_ref[...] = jnp.dot(h.astype(BF16), win_ref[...], preferred_element_type=F32)

    for c in range(tm // chunk):
        rows = slice(c * chunk, (c + 1) * chunk)
        cq, sq, ck, sk = cq_ref[rows, :], sq_ref[rows, :], ck_ref[rows, :], sk_ref[rows, :]
        for hd in range(RET_HEADS):
            lo = hd * RET_HEAD_DIM
            q = z_ref[rows, lo:lo + RET_HEAD_DIM]
            k = z_ref[rows, RET_WIDTH + lo:RET_WIDTH + lo + RET_HEAD_DIM]
            v = z_ref[rows, 2 * RET_WIDTH + lo:2 * RET_WIDTH + lo + RET_HEAD_DIM]
            g = z_ref[rows, 3 * RET_WIDTH + lo:3 * RET_WIDTH + lo + RET_HEAD_DIM]
            qr = q * cq + pltpu.roll(q, RET_HEAD_DIM // 2, 1) * sq
            kr = k * ck + pltpu.roll(k, RET_HEAD_DIM // 2, 1) * sk
            qb, vb = qr.astype(BF16), v.astype(BF16)
            sc = lax.dot_general(qb, kr.astype(BF16), NT_DIMS, preferred_element_type=F32) * dmask_ref[hd]
            s_prev = s_ref[hd]
            o = (jnp.dot(sc.astype(BF16), vb, preferred_element_type=F32)
                 + jnp.dot(qb, s_prev.astype(BF16), preferred_element_type=F32) * cross_ref[hd])
            kw = (kr * kdec_ref[hd]).astype(BF16)
            s_ref[hd] = sdecay[hd] * s_prev + lax.dot_general(kw, vb, TN_DIMS, preferred_element_type=F32)
            on = _center_norm(o) * gn_ref[:, lo:lo + RET_HEAD_DIM]
            ycat_ref[rows, lo:lo + RET_HEAD_DIM] = (g * jax.nn.sigmoid(g) * on).astype(BF16)

    a = z_ref[:, 4 * RET_WIDTH:4 * RET_WIDTH + CONV_WIDTH]
    b = z_ref[:, 4 * RET_WIDTH + CONV_WIDTH:IN_WIDTH]
    ext_ref[HIST_BASE:HIST_BASE + tm, :] = a * jax.nn.sigmoid(b)
    first = HIST_BASE - CONV_HIST
    taps = []
    for shift in range(SUBLANES):
        js = [j for j in range(CONV_KERNEL) if (first + j) % SUBLANES == shift]
        bases = [first + j - shift for j in js]
        if shift == 0:
            src = ext_ref
        else:
            src = shift_ref.at[shift - 1]
            lo, hi = min(bases), max(bases) + tm
            src[lo:hi, :] = ext_ref[lo + shift:hi + shift, :]
        taps += [(j, src, base) for j, base in zip(js, bases)]
    for r in range(tm // CONV_ROWS):
        acc = jnp.broadcast_to(cb_ref[...], (CONV_ROWS, CONV_WIDTH))
        for j, src, base in taps:
            start = base + r * CONV_ROWS
            acc = acc + cw_ref[j:j + 1, :] * src[start:start + CONV_ROWS, :]
        cn = _center_norm(acc) * lng_ref[...] + lnb_ref[...]
        ycat_ref[r * CONV_ROWS:(r + 1) * CONV_ROWS, RET_WIDTH:] = (cn * jax.nn.sigmoid(cn)).astype(BF16)
    hist = ext_ref[first + tm:HIST_BASE + tm, :]
    ext_ref[first:HIST_BASE, :] = hist

    @pl.when(i == last)
    def _():
        convn_ref[...] = hist
        retn_ref[...] = s_ref[...]

    x1_ref[...] = x + jnp.dot(ycat_ref[...], wout_ref[...], preferred_element_type=F32)


def _const_spec(shape):
    return pl.BlockSpec(shape, lambda b, i: (0,) * len(shape))


def _mixer(x, ret0, conv0, pos0, w, *, tm, chunk):
    bsz, seq, _ = x.shape
    assert seq % tm == 0 and tm % chunk == 0 and tm % CONV_ROWS == 0 and tm >= CONV_HIST
    half = RET_HEAD_DIM // 2
    inv_freq = ROPE_BASE ** (-jnp.arange(half, dtype=F32) / half)
    ang = (pos0 + jnp.arange(seq, dtype=F32))[:, None] * inv_freq[None, :]
    cos, sin = jnp.cos(ang), jnp.sin(ang)
    cos2 = jnp.concatenate([cos, cos], axis=-1)
    sin2 = jnp.concatenate([-sin, sin], axis=-1)
    kscale = RET_HEAD_DIM ** -0.5
    log_gamma = jnp.log1p(-jnp.exp2(-5.0 - jnp.arange(RET_HEADS, dtype=F32)))
    idx = jnp.arange(chunk, dtype=F32)
    diff = idx[:, None] - idx[None, :]
    causal = diff >= 0
    dmask = jnp.where(causal[None], jnp.exp(log_gamma[:, None, None] * jnp.where(causal, diff, 0.0)[None]), 0.0)
    cross = jnp.broadcast_to(jnp.exp(log_gamma[:, None] * (idx[None, :] + 1.0))[:, :, None],
                             (RET_HEADS, chunk, RET_HEAD_DIM))
    kdec = jnp.broadcast_to(jnp.exp(log_gamma[:, None] * (chunk - 1.0 - idx[None, :]))[:, :, None],
                            (RET_HEADS, chunk, RET_HEAD_DIM))
    lg64 = np.log1p(-np.exp2(-5.0 - np.arange(RET_HEADS, dtype=np.float64)))
    sdecay = tuple(float(np.float32(np.exp(lg64[h] * chunk))) for h in range(RET_HEADS))

    tok_spec = pl.BlockSpec((None, tm, D_MODEL), lambda b, i: (b, i, 0))
    tab_spec = pl.BlockSpec((tm, RET_HEAD_DIM), lambda b, i: (i, 0))
    ret_spec = pl.BlockSpec((None, RET_HEADS, RET_HEAD_DIM, RET_HEAD_DIM), lambda b, i: (b, 0, 0, 0))
    conv_spec = pl.BlockSpec((None, CONV_HIST, CONV_WIDTH), lambda b, i: (b, 0, 0))
    return pl.pallas_call(
        functools.partial(_mixer_kernel, tm=tm, chunk=chunk, sdecay=sdecay),
        grid=(bsz, seq // tm),
        in_specs=[tok_spec, ret_spec, conv_spec,
                  _const_spec((1, D_MODEL)), _const_spec((D_MODEL, IN_WIDTH)),
                  tab_spec, tab_spec, tab_spec, tab_spec,
                  _const_spec((RET_HEADS, chunk, chunk)),
                  _const_spec((RET_HEADS, chunk, RET_HEAD_DIM)), _const_spec((RET_HEADS, chunk, RET_HEAD_DIM)),
                  _const_spec((1, RET_WIDTH)), _const_spec((CONV_KERNEL, CONV_WIDTH)),
                  _const_spec((1, CONV_WIDTH)), _const_spec((1, CONV_WIDTH)), _const_spec((1, CONV_WIDTH)),
                  _const_spec((D_MODEL, D_MODEL))],
        out_specs=[tok_spec, ret_spec, conv_spec],
        out_shape=[jax.ShapeDtypeStruct((bsz, seq, D_MODEL), F32),
                   jax.ShapeDtypeStruct((bsz, RET_HEADS, RET_HEAD_DIM, RET_HEAD_DIM), F32),
                   jax.ShapeDtypeStruct((bsz, CONV_HIST, CONV_WIDTH), F32)],
        scratch_shapes=[pltpu.VMEM((tm, IN_WIDTH), F32),
                        pltpu.VMEM((RET_HEADS, RET_HEAD_DIM, RET_HEAD_DIM), F32),
                        pltpu.VMEM((HIST_BASE + tm, CONV_WIDTH), F32),
                        pltpu.VMEM((SUBLANES - 1, HIST_BASE + tm, CONV_WIDTH), F32),
                        pltpu.VMEM((tm, D_MODEL), BF16)],
        compiler_params=pltpu.CompilerParams(dimension_semantics=("arbitrary", "arbitrary"),
                                             vmem_limit_bytes=VMEM_LIMIT),
        name="mixer",
    )(x, ret0, conv0, w["norm_mix"], w["w_in"], cos2, sin2, cos2 * kscale, sin2 * kscale,
      dmask, cross, kdec, w["ret_gn_gain"], w["conv_dw"], w["conv_dw_bias"], w["conv_ln_gain"],
      w["conv_ln_bias"], w["w_out"])


def _top16_distinct(sc, want_rank):
    s = sc
    rank = jnp.full(sc.shape, float(PEER_TOPK), F32)
    tops = []
    for k in range(PEER_TOPK):
        m = jnp.max(s, axis=0, keepdims=True)
        hit = s == m
        s = jnp.where(hit, -jnp.inf, s)
        if want_rank:
            rank = jnp.where(hit, float(k), rank)
        tops.append(m)
    removed = jnp.sum(jnp.where(s == -jnp.inf, 1.0, 0.0), axis=0, keepdims=True)
    return jnp.concatenate(tops, axis=0), removed, rank


def _top16_ranked(sc, iota):
    s = sc
    rank = jnp.full(sc.shape, float(PEER_TOPK), F32)
    tops = []
    for k in range(PEER_TOPK):
        m = jnp.max(s, axis=0, keepdims=True)
        idx = jnp.min(jnp.where(s == m, iota, float(PEER_KEYS)), axis=0, keepdims=True)
        hit = iota == idx
        s = jnp.where(hit, -jnp.inf, s)
        rank = jnp.where(hit, float(k), rank)
        tops.append(m)
    return jnp.concatenate(tops, axis=0), rank


def _merge16(a, b, iota16):
    count = jnp.zeros_like(a)
    front = a + b[0:1, :]
    best0 = None
    norm = None
    for k in range(PEER_TOPK):
        m = jnp.max(front, axis=0, keepdims=True)
        isel = jnp.min(jnp.where(front == m, iota16, float(PEER_TOPK)), axis=0, keepdims=True)
        hit = iota16 == isel
        if k == 0:
            best0 = m
            norm = jnp.ones_like(m)
        else:
            norm = norm + jnp.exp(m - best0)
        count = count + jnp.where(hit, 1.0, 0.0)
        if k + 1 < PEER_TOPK:
            nxt = jnp.sum(jnp.where(hit, count, 0.0), axis=0, keepdims=True)
            b_nxt = jnp.sum(jnp.where(iota16 == nxt, b, 0.0), axis=0, keepdims=True)
            b_nxt = jnp.where(nxt >= float(PEER_TOPK), -jnp.inf, b_nxt)
            front = jnp.where(hit, a + b_nxt, front)
    return count, norm


def _route(qT_ref, keys_ref, ra_ref, ea_ref, rb_ref, eb_ref, *, tm):
    iota16 = lax.broadcasted_iota(jnp.int32, (PEER_TOPK, LANES), 0).astype(F32)
    blocks = [slice(sb * LANES, (sb + 1) * LANES) for sb in range(tm // LANES)]

    def scores(hd, half, lanes):
        row0 = pl.multiple_of((hd * 2 + half) * PEER_KEY_DIM, PEER_KEY_DIM)
        qs = qT_ref[pl.ds(row0, PEER_KEY_DIM), lanes].astype(BF16)
        return jnp.dot(keys_ref[hd * 2 + half], qs, preferred_element_type=F32)

    def write_tables(hd, lanes, sc_a, sc_b, top_a, top_b, taken, rank_b, norm):
        tiled = ra_ref.shape[1:3]
        ra_ref[hd, :, :, lanes] = taken.reshape(*tiled, LANES)
        ea_ref[hd, :, :, lanes] = (jnp.exp(sc_a - top_a[0:1, :]) / norm).reshape(*tiled, LANES)
        rb_ref[hd, :, lanes] = rank_b.astype(BF16)
        eb_ref[hd, :, lanes] = jnp.exp(sc_b - top_b[0:1, :]).astype(BF16)

    def ranked_unit(hd, lanes):
        iota = lax.broadcasted_iota(jnp.int32, (PEER_KEYS, LANES), 0).astype(F32)
        sc_a, sc_b = scores(hd, 0, lanes), scores(hd, 1, lanes)
        top_a, rank_a = _top16_ranked(sc_a, iota)
        top_b, rank_b = _top16_ranked(sc_b, iota)
        count, norm = _merge16(top_a, top_b, iota16)
        taken = jnp.zeros(sc_a.shape, F32)
        for k in range(PEER_TOPK):
            taken = jnp.where(rank_a == float(k), count[k:k + 1, :], taken)
        write_tables(hd, lanes, sc_a, sc_b, top_a, top_b, taken, rank_b, norm)

    def head_body(hd, carry):
        tied = []
        for lanes in blocks:
            sc_a, sc_b = scores(hd, 0, lanes), scores(hd, 1, lanes)
            top_a, removed_a, _ = _top16_distinct(sc_a, False)
            top_b, removed_b, rank_b = _top16_distinct(sc_b, True)
            count, norm = _merge16(top_a, top_b, iota16)
            taken = jnp.zeros(sc_a.shape, F32)
            for k in range(PEER_TOPK):
                taken = jnp.where(sc_a == top_a[k:k + 1, :], count[k:k + 1, :], taken)
            write_tables(hd, lanes, sc_a, sc_b, top_a, top_b, taken, rank_b, norm)
            tied.append(jnp.max(jnp.maximum(removed_a, removed_b)) > float(PEER_TOPK))
        for lanes, tie in zip(blocks, tied):
            pl.when(tie)(functools.partial(ranked_unit, hd, lanes))
        return carry

    lax.fori_loop(0, PEER_HEADS, head_body, 0)


def _ffn_kernel(x1_ref, nffn_ref, wqT_ref, keys_ref, down_hbm, upT_hbm, nfin_ref, y_ref,
                h2_ref, ra_ref, ea_ref, rb_ref, eb_ref, act_ref, wT_ref, acc_ref, *, tm, te):
    def tile_body(down_ref, upT_ref):
        j = pl.program_id(0)

        @pl.when(j == 0)
        def _():
            h2 = _rms(x1_ref[...], nffn_ref[...]).astype(BF16)
            h2_ref[...] = h2
            act_ref[0:N_QUERY, :] = lax.dot_general(wqT_ref[...], h2, NT_DIMS, preferred_element_type=F32)
            _route(act_ref, keys_ref, ra_ref, ea_ref, rb_ref, eb_ref, tm=tm)
            acc_ref[...] = jnp.zeros_like(acc_ref)

        h2 = h2_ref[...]
        grp_rows = GROUP_FIRST * PEER_KEYS
        groups = [slice(g * grp_rows, (g + 1) * grp_rows) for g in range(te // grp_rows)]
        for rows in groups:
            act_ref[rows, :] = lax.dot_general(down_ref[rows, :], h2, NT_DIMS, preferred_element_type=F32)
        for g, rows in enumerate(groups):
            for sb in range(tm // LANES):
                lanes = slice(sb * LANES, (sb + 1) * LANES)
                for a in range(GROUP_FIRST):
                    r = g * GROUP_FIRST + a
                    thr = [jnp.broadcast_to(ra_ref[hd, j, r:r + 1, lanes], (BF16_ROWS, LANES)).astype(BF16)
                           for hd in range(PEER_HEADS)]
                    gat = [jnp.broadcast_to(ea_ref[hd, j, r:r + 1, lanes], (BF16_ROWS, LANES)).astype(BF16)
                           for hd in range(PEER_HEADS)]
                    for slab in range(PEER_KEYS // BF16_ROWS):
                        keys = slice(slab * BF16_ROWS, (slab + 1) * BF16_ROWS)
                        gate = jnp.zeros((BF16_ROWS, LANES), BF16)
                        for hd in range(PEER_HEADS):
                            hit = rb_ref[hd, keys, lanes] < thr[hd]
                            gate = gate + jnp.where(hit, eb_ref[hd, keys, lanes], 0.0) * gat[hd]
                        row0 = rows.start + a * PEER_KEYS + slab * BF16_ROWS
                        act = _gelu_tanh(act_ref[row0:row0 + BF16_ROWS, lanes])
                        wT_ref[row0:row0 + BF16_ROWS, lanes] = gate * act.astype(BF16)
            acc_ref[...] += jnp.dot(upT_ref[:, rows], wT_ref[rows, :], preferred_element_type=F32)

    pltpu.emit_pipeline(
        tile_body, grid=(PEER_EXPERTS // te,),
        in_specs=[pl.BlockSpec((te, D_MODEL), lambda j: (j, 0)), pl.BlockSpec((D_MODEL, te), lambda j: (0, j))],
    )(down_hbm, upT_hbm)
    y_ref[...] = _rms(x1_ref[...] + acc_ref[...].T, nfin_ref[...])


def _ffn(x1, w, *, tm, te):
    shape = x1.shape
    x1 = x1.reshape(-1, D_MODEL)
    t = x1.shape[0]
    assert t % tm == 0 and tm % LANES == 0 and PEER_EXPERTS % te == 0
    assert te % (SUBLANES * PEER_KEYS) == 0 and SUBLANES % GROUP_FIRST == 0
    tok_spec = pl.BlockSpec((tm, D_MODEL), lambda i: (i, 0))
    tab = (PEER_HEADS, PEER_KEYS, tm)
    tab_first = (PEER_HEADS, PEER_EXPERTS // te, te // PEER_KEYS, tm)
    return pl.pallas_call(
        functools.partial(_ffn_kernel, tm=tm, te=te),
        grid=(t // tm,),
        in_specs=[tok_spec,
                  pl.BlockSpec((1, D_MODEL), lambda i: (0, 0)),
                  pl.BlockSpec((N_QUERY, D_MODEL), lambda i: (0, 0)),
                  pl.BlockSpec((PEER_HEADS * 2, PEER_KEYS, PEER_KEY_DIM), lambda i: (0, 0, 0)),
                  pl.BlockSpec(memory_space=pl.ANY),
                  pl.BlockSpec(memory_space=pl.ANY),
                  pl.BlockSpec((1, D_MODEL), lambda i: (0, 0))],
        out_specs=tok_spec,
        out_shape=jax.ShapeDtypeStruct((t, D_MODEL), F32),
        scratch_shapes=[pltpu.VMEM((tm, D_MODEL), BF16),
                        pltpu.VMEM(tab_first, F32), pltpu.VMEM(tab_first, F32),
                        pltpu.VMEM(tab, BF16), pltpu.VMEM(tab, BF16),
                        pltpu.VMEM((max(te, N_QUERY), tm), F32), pltpu.VMEM((te, tm), BF16),
                        pltpu.VMEM((D_MODEL, tm), F32)],
        compiler_params=pltpu.CompilerParams(dimension_semantics=("arbitrary",),
                                             vmem_limit_bytes=VMEM_LIMIT),
        name="ffn",
    )(x1, w["norm_ffn"], w["peer_query_t"], w["peer_keys"], w["peer_down"], w["peer_up_t"],
      w["norm_final"]).reshape(shape)


def kernel(x_prompt, x_sample, state_retention, state_conv, norm_mix, w_in, ret_gn_gain, conv_dw, conv_dw_bias,
           conv_ln_gain, conv_ln_bias, w_out, norm_ffn, peer_query, peer_sub_keys, peer_down, peer_up, norm_final):
    assert w_in.shape[0] == 1, "single-layer model"
    row = lambda v: v.reshape(1, -1).astype(F32)
    w = {
        "norm_mix": row(norm_mix[0]), "w_in": w_in[0].astype(BF16), "ret_gn_gain": row(ret_gn_gain[0]),
        "conv_dw": conv_dw[0].astype(F32), "conv_dw_bias": row(conv_dw_bias[0]),
        "conv_ln_gain": row(conv_ln_gain[0]), "conv_ln_bias": row(conv_ln_bias[0]),
        "w_out": w_out[0].astype(BF16), "norm_ffn": row(norm_ffn[0]),
        "peer_query_t": peer_query[0].T.astype(BF16),
        "peer_keys": peer_sub_keys[0].reshape(PEER_HEADS * 2, PEER_KEYS, PEER_KEY_DIM).astype(BF16),
        "peer_down": peer_down[0].astype(BF16), "peer_up_t": peer_up[0].T.astype(BF16),
        "norm_final": row(norm_final),
    }
    bp = x_prompt.shape[0]
    ret0 = jnp.zeros((bp, RET_HEADS, RET_HEAD_DIM, RET_HEAD_DIM), F32)
    conv0 = jnp.zeros((bp, CONV_HIST, CONV_WIDTH), F32)
    x1p, ret_p, conv_p = _mixer(x_prompt, ret0, conv0, 0.0, w, tm=512, chunk=256)
    x1s, ret_s, conv_s = _mixer(x_sample, state_retention[0], state_conv[0], float(PAST_LEN), w,
                                tm=x_sample.shape[1], chunk=x_sample.shape[1])
    y_prompt = _ffn(x1p, w, tm=512, te=2048)
    y_sample = _ffn(x1s, w, tm=256, te=2048)
    return (y_prompt, y_sample, ret_p[None], ret_s[None], conv_p[None], conv_s[None])
```
